```python
import jax, jax.numpy as jnp
from jax import lax
import numpy as np

D_MODEL = 2048
BATCH = 4
SEQ = 2048
DEPTH = 4

HEAD_DIM = 128
N_HEADS_A = (D_MODEL // 2) // HEAD_DIM
C_B = D_MODEL // 2
N_HEADS_C = D_MODEL // HEAD_DIM
ROT_DIM = HEAD_DIM // 4
ROPE_THETA = 500000.0
DILATED_BRANCHES = ((128, 1), (512, 4), (2048, 16))
CONV_B_WIDTH = 31
FFN_CONV_WIDTH = 3
D_FF = ((8 * D_MODEL // 3 + 255) // 256) * 256
Q_BLOCK = 128
N_EVEN = (DEPTH + 1) // 2
N_ODD = DEPTH // 2
EPS = 1e-6
EVEN_IN = 3 * N_HEADS_A * HEAD_DIM + 2 * C_B
ODD_IN = 3 * N_HEADS_C * HEAD_DIM + N_HEADS_C

kernel_name = "hybrid_dilated_conformer_fox_convffn"

F32 = jnp.float32


def rmsnorm(x, g):
    xf = x.astype(F32)
    y = xf * lax.rsqrt(jnp.mean(xf * xf, axis=-1, keepdims=True) + EPS)
    return (y * g.astype(F32)).astype(x.dtype)


def layernorm(x, g, b):
    xf = x.astype(F32)
    mu = jnp.mean(xf, axis=-1, keepdims=True)
    var = jnp.mean(jnp.square(xf - mu), axis=-1, keepdims=True)
    y = (xf - mu) * lax.rsqrt(var + EPS)
    return (y * g.astype(F32) + b.astype(F32)).astype(x.dtype)


def causal_dwconv(x, w):
    K = w.shape[0]
    return lax.conv_general_dilated(
        x, w[:, None, :].astype(x.dtype), window_strides=(1,),
        padding=[(K - 1, 0)], dimension_numbers=("NWC", "WIO", "NWC"),
        feature_group_count=x.shape[-1])


def partial_rope(x, positions):
    half = ROT_DIM // 2
    inv = jnp.power(jnp.float32(ROPE_THETA), -jnp.arange(half, dtype=F32) * (2.0 / ROT_DIM))
    ang = positions.astype(F32)[..., None] * inv
    cos = jnp.cos(ang)[:, :, None, :]
    sin = jnp.sin(ang)[:, :, None, :]
    x1 = x[..., :half].astype(F32)
    x2 = x[..., half:ROT_DIM].astype(F32)
    r1 = (x1 * cos - x2 * sin).astype(x.dtype)
    r2 = (x2 * cos + x1 * sin).astype(x.dtype)
    return jnp.concatenate([r1, r2, x[..., ROT_DIM:]], axis=-1)


def dilated_branch(q, k, v, window, dilation):
    B, S, H, E = q.shape
    W = window // dilation
    L = S // dilation
    nb = -(-L // W)
    Lp = nb * W

    def to_sub(t):
        t = t.reshape(B, L, dilation, H, E).transpose(0, 2, 3, 1, 4)
        t = jnp.pad(t, ((0, 0), (0, 0), (0, 0), (0, Lp - L), (0, 0)))
        return t.reshape(B, dilation, H, nb, W, E)

    def with_prev(t):
        prev = jnp.pad(t, ((0, 0), (0, 0), (0, 0), (1, 0), (0, 0), (0, 0)))[:, :, :, :-1]
        return jnp.concatenate([prev, t], axis=4)

    qs = to_sub(q)
    kk = with_prev(to_sub(k))
    vv = with_prev(to_sub(v))
    s = jnp.einsum("brhnqe,brhnke->brhnqk", qs, kk, preferred_element_type=F32)
    i = jnp.arange(W)[:, None]
    j = jnp.arange(2 * W)[None, :]
    dist = W + i - j
    band = (dist >= 0) & (dist <= W)
    not_pad = (jnp.arange(nb)[:, None, None] > 0) | (j >= W)[None]
    mask = band[None] & not_pad
    s = jnp.where(mask, s, -jnp.inf)
    m = jnp.max(s, axis=-1, keepdims=True)
    p = jnp.exp(s - m)
    den = jnp.sum(p, axis=-1, keepdims=True)
    o = jnp.einsum("brhnqk,brhnke->brhnqe", (p / den).astype(v.dtype), vv,
                   preferred_element_type=F32)
    lse = (m + jnp.log(den))[..., 0]

    def from_sub(t):
        tail = t.shape[5:]
        t = t.reshape((B, dilation, H, Lp) + tail)[:, :, :, :L]
        perm = (0, 3, 1, 2) + tuple(range(4, t.ndim))
        return t.transpose(perm).reshape((B, S, H) + tail)

    return from_sub(o), from_sub(lse)


def dilated_attention(q, k, v):
    outs, lses = [], []
    for window, dilation in DILATED_BRANCHES:
        o, l = dilated_branch(q, k, v, window, dilation)
        outs.append(o)
        lses.append(l)
    w = jax.nn.softmax(jnp.stack(lses, axis=0), axis=0)
    return jnp.sum(w[..., None] * jnp.stack(outs, axis=0), axis=0)


def forgetting_attention(q, k, v, logf):
    B, S, H, E = q.shape
    nb = S // Q_BLOCK
    Fc = jnp.cumsum(logf, axis=1)
    Ft = Fc.transpose(0, 2, 1)
    qb = q.reshape(B, nb, Q_BLOCK, H, E).transpose(1, 0, 2, 3, 4)
    Fb = Ft.reshape(B, H, nb, Q_BLOCK).transpose(2, 0, 1, 3)
    pos = jnp.arange(S)
    idxb = pos.reshape(nb, Q_BLOCK)

    def block(args):
        qi, Fi, ti = args
        s = jnp.einsum("bqhe,bkhe->bhqk", qi, k, preferred_element_type=F32)
        s = s + (Fi[..., :, None] - Ft[..., None, :])
        s = jnp.where(ti[:, None] >= pos[None, :], s, -jnp.inf)
        p = jax.nn.softmax(s, axis=-1)
        return jnp.einsum("bhqk,bkhe->bqhe", p.astype(v.dtype), v, preferred_element_type=F32)

    o = lax.map(block, (qb, Fb, idxb))
    return o.transpose(1, 0, 2, 3, 4).reshape(B, S, H, E)


def even_mixer(h, positions, w_in, conv_w, conv_b, ln_g, ln_b, w_out):
    B, S, _ = h.shape
    dA = N_HEADS_A * HEAD_DIM
    u = h @ w_in
    q, k, v, a, gate = jnp.split(u, [dA, 2 * dA, 3 * dA, 3 * dA + C_B], axis=-1)
    q = partial_rope(q.reshape(B, S, N_HEADS_A, HEAD_DIM), positions) * (HEAD_DIM ** -0.5)
    k = partial_rope(k.reshape(B, S, N_HEADS_A, HEAD_DIM), positions)
    v = v.reshape(B, S, N_HEADS_A, HEAD_DIM)
    y_a = dilated_attention(q, k, v).reshape(B, S, dA).astype(h.dtype)
    g = a * jax.nn.sigmoid(gate)
    g = causal_dwconv(g, conv_w) + conv_b
    y_b = jax.nn.silu(layernorm(g, ln_g, ln_b))
    return jnp.concatenate([y_a, y_b], axis=-1) @ w_out


def odd_mixer(h, w_in, b_f, w_out):
    B, S, _ = h.shape
    dC = N_HEADS_C * HEAD_DIM
    u = h @ w_in
    q, k, v, fl = jnp.split(u, [dC, 2 * dC, 3 * dC], axis=-1)
    q = q.reshape(B, S, N_HEADS_C, HEAD_DIM) * (HEAD_DIM ** -0.5)
    k = k.reshape(B, S, N_HEADS_C, HEAD_DIM)
    v = v.reshape(B, S, N_HEADS_C, HEAD_DIM)
    logf = jax.nn.log_sigmoid((fl + b_f).astype(F32))
    y = forgetting_attention(q, k, v, logf).reshape(B, S, dC).astype(h.dtype)
    return y @ w_out


def conv_ffn(h, w_up, conv_w, w_down):
    up = h @ w_up
    g, u = jnp.split(up, [D_FF], axis=-1)
    g = causal_dwconv(g, conv_w)
    return (jax.nn.silu(g) * u) @ w_down


def setup_inputs(seed: int = 0) -> dict:
    key = jax.random.key(seed)
    ks = jax.random.split(key, 20)
    nrm = lambda k, shape, scale: jax.random.normal(k, shape, F32) * scale
    x = jax.random.normal(ks[0], (BATCH, SEQ, D_MODEL), F32)
    offs = jax.random.randint(ks[1], (BATCH, 1), 0, 4096, dtype=jnp.int32)
    positions = jnp.arange(SEQ, dtype=jnp.int32)[None, :] + offs
    dA = N_HEADS_A * HEAD_DIM
    dC = N_HEADS_C * HEAD_DIM
    return {
        "x": x,
        "positions": positions,
        "norm_mix": 1.0 + nrm(ks[2], (DEPTH, D_MODEL), 0.02),
        "norm_ffn": 1.0 + nrm(ks[3], (DEPTH, D_MODEL), 0.02),
        "norm_final": 1.0 + nrm(ks[4], (D_MODEL,), 0.02),
        "ev_w_in": nrm(ks[5], (N_EVEN, D_MODEL, EVEN_IN), D_MODEL ** -0.5),
        "ev_conv_w": nrm(ks[6], (N_EVEN, CONV_B_WIDTH, C_B), CONV_B_WIDTH ** -0.5),
        "ev_conv_b": nrm(ks[7], (N_EVEN, C_B), 0.02),
        "ev_ln_g": 1.0 + nrm(ks[8], (N_EVEN, C_B), 0.02),
        "ev_ln_b": nrm(ks[9], (N_EVEN, C_B), 0.02),
        "ev_w_out": nrm(ks[10], (N_EVEN, dA + C_B, D_MODEL), (dA + C_B) ** -0.5),
        "od_w_in": nrm(ks[11], (N_ODD, D_MODEL, ODD_IN), D_MODEL ** -0.5),
        "od_b_f": nrm(ks[12], (N_ODD, N_HEADS_C), 0.1),
        "od_w_out": nrm(ks[13], (N_ODD, dC, D_MODEL), dC ** -0.5),
        "ffn_w_up": nrm(ks[14], (DEPTH, D_MODEL, 2 * D_FF), D_MODEL ** -0.5),
        "ffn_conv_w": nrm(ks[15], (DEPTH, FFN_CONV_WIDTH, D_FF), FFN_CONV_WIDTH ** -0.5),
        "ffn_w_down": nrm(ks[16], (DEPTH, D_FF, D_MODEL), D_FF ** -0.5),
    }


def reference(x, positions, norm_mix, norm_ffn, norm_final, ev_w_in, ev_conv_w, ev_conv_b,
              ev_ln_g, ev_ln_b, ev_w_out, od_w_in, od_b_f, od_w_out, ffn_w_up, ffn_conv_w,
              ffn_w_down):
    for l in range(DEPTH):
        h = rmsnorm(x, norm_mix[l])
        if l % 2 == 0:
            e = l // 2
            x = x + even_mixer(h, positions, ev_w_in[e], ev_conv_w[e], ev_conv_b[e],
                               ev_ln_g[e], ev_ln_b[e], ev_w_out[e])
        else:
            o = l // 2
            x = x + odd_mixer(h, od_w_in[o], od_b_f[o], od_w_out[o])
        h = rmsnorm(x, norm_ffn[l])
        x = x + conv_ffn(h, ffn_w_up[l], ffn_conv_w[l], ffn_w_down[l])
    return rmsnorm(x, norm_final)
```

```python
import functools

import jax
import jax.numpy as jnp
from jax import lax
from jax.experimental import pallas as pl
from jax.experimental.pallas import tpu as pltpu

F32 = jnp.float32
BF16 = jnp.bfloat16

HEAD_DIM = 128
ROT_DIM = 32
ROPE_THETA = 500000.0
CONV_B_WIDTH = 31
CONV_HALO = 32
FFN_CONV_WIDTH = 3
EPS = 1e-6
NEG = -1e30
DIL_W = 128
LANES = 128
SUBLANES = 8
VMEM_LIMIT = 56 * 1024 * 1024

_NT = (((1,), (1,)), ((), ()))


def _cparams(sem):
    return pltpu.CompilerParams(dimension_semantics=sem, vmem_limit_bytes=VMEM_LIMIT)


def _rmsnorm_kernel(x_ref, g_ref, o_ref):
    x = x_ref[...]
    ms = jnp.mean(x * x, axis=-1, keepdims=True)
    o_ref[...] = (x * lax.rsqrt(ms + EPS) * g_ref[...]).astype(o_ref.dtype)


def rmsnorm(x, g, out_dtype):
    m, d = x.shape
    tm = 512
    return pl.pallas_call(
        _rmsnorm_kernel,
        grid=(m // tm,),
        in_specs=[pl.BlockSpec((tm, d), lambda i: (i, 0)),
                  pl.BlockSpec((1, d), lambda i: (0, 0))],
        out_specs=pl.BlockSpec((tm, d), lambda i: (i, 0)),
        out_shape=jax.ShapeDtypeStruct((m, d), out_dtype),
        compiler_params=_cparams(("parallel",)),
        name="rmsnorm",
    )(x, g.reshape(1, d))


def _mm_kernel(*refs, n_ops, has_res, scale, n_scale_tiles):
    a_refs = refs[:n_ops]
    w_refs = refs[n_ops:2 * n_ops]
    res_ref = refs[2 * n_ops] if has_res else None
    o_ref = refs[-1]
    acc = jnp.dot(a_refs[0][...], w_refs[0][...], preferred_element_type=F32)
    for a_ref, w_ref in zip(a_refs[1:], w_refs[1:]):
        acc = acc + jnp.dot(a_ref[...], w_ref[...], preferred_element_type=F32)
    if scale is not None:
        acc = acc * jnp.where(pl.program_id(0) < n_scale_tiles, scale, 1.0).astype(F32)
    if has_res:
        acc = acc + res_ref[...]
    o_ref[...] = acc.astype(o_ref.dtype)


def matmul(a_list, w, n_cols, out_dtype, *, res=None, tm, tn, scale=None, scale_cols=0):
    m = a_list[0].shape[0]
    kk = a_list[0].shape[1]
    n_ops = len(a_list)
    assert all(a.shape == (m, kk) for a in a_list) and w.shape[0] == n_ops * kk
    assert m % tm == 0 and n_cols % tn == 0 and scale_cols % tn == 0
    in_specs = [pl.BlockSpec((tm, kk), lambda j, i: (i, 0)) for _ in a_list]
    in_specs += [pl.BlockSpec((kk, tn), functools.partial(lambda j, i, r: (r, j), r=r))
                 for r in range(n_ops)]
    args = list(a_list) + [w] * n_ops
    if res is not None:
        in_specs.append(pl.BlockSpec((tm, tn), lambda j, i: (i, j)))
        args.append(res)
    kern = functools.partial(_mm_kernel, n_ops=n_ops, has_res=res is not None, scale=scale,
                             n_scale_tiles=scale_cols // tn)
    return pl.pallas_call(
        kern,
        grid=(n_cols // tn, m // tm),
        in_specs=in_specs,
        out_specs=pl.BlockSpec((tm, tn), lambda j, i: (i, j)),
        out_shape=jax.ShapeDtypeStruct((m, n_cols), out_dtype),
        compiler_params=_cparams(("parallel", "parallel")),
        name="matmul",
    )(*args)


def _rope_table_kernel(pos_ref, inv_ref, c_ref, sa_ref, sb_ref):
    ang = pos_ref[...].astype(F32) * inv_ref[...]
    lane = lax.broadcasted_iota(jnp.int32, ang.shape, 1)
    cos = jnp.cos(ang)
    sin = jnp.sin(ang)
    half = ROT_DIM // 2
    c_ref[...] = jnp.where(lane < ROT_DIM, cos, 1.0)
    sa_ref[...] = jnp.where(lane < half, -sin, 0.0)
    sb_ref[...] = jnp.where(lane < ROT_DIM, jnp.where(lane >= half, sin, 0.0), 0.0)


def rope_tables(positions):
    m = positions.size
    half = ROT_DIM // 2
    inv = jnp.power(jnp.float32(ROPE_THETA), -jnp.arange(half, dtype=F32) * (2.0 / ROT_DIM))
    inv_lane = jnp.concatenate([inv, inv, jnp.zeros((LANES - ROT_DIM,), F32)]).reshape(1, LANES)
    tm = 1024
    shp = jax.ShapeDtypeStruct((m, LANES), F32)
    spec = pl.BlockSpec((tm, LANES), lambda i: (i, 0))
    return pl.pallas_call(
        _rope_table_kernel,
        grid=(m // tm,),
        in_specs=[pl.BlockSpec((tm, 1), lambda i: (i, 0)),
                  pl.BlockSpec((1, LANES), lambda i: (0, 0))],
        out_specs=[spec, spec, spec],
        out_shape=[shp, shp, shp],
        compiler_params=_cparams(("parallel",)),
        name="rope_tables",
    )(positions.reshape(m, 1), inv_lane)


def _dilated_kernel(q_ref, k_ref, v_ref, c_ref, sa_ref, sb_ref, o_ref,
                    qr, kr, o1, o2, o3, l1, l2, l3, *, seq):
    half = ROT_DIM // 2
    c = c_ref[...]
    sa = sa_ref[...]
    sb = sb_ref[...]

    def rope(x):
        return x * c + pltpu.roll(x, LANES - half, 1) * sa + pltpu.roll(x, half, 1) * sb

    qr[...] = rope(q_ref[...]) * (HEAD_DIM ** -0.5)
    kr[...] = rope(k_ref[...])

    def block(q_start, k_start, nk, stride, o_dst, l_dst):
        def sl(start, n):
            return pl.ds(start, n) if stride == 1 else pl.ds(start, n, stride=stride)
        qs = qr[sl(q_start, DIL_W), :].astype(BF16)
        ks = kr[sl(k_start, nk), :].astype(BF16)
        vs = v_ref[sl(k_start, nk), :].astype(BF16)
        s = lax.dot_general(qs, ks, _NT, preferred_element_type=F32)
        i = lax.broadcasted_iota(jnp.int32, s.shape, 0)
        j = lax.broadcasted_iota(jnp.int32, s.shape, 1)
        dist = (nk - DIL_W) + i - j
        if nk == DIL_W:
            s = jnp.where(dist >= 0, s, NEG)
        else:
            s = jnp.where(dist >= 0, jnp.where(dist <= DIL_W, s, NEG), NEG)
        m = jnp.max(s, axis=-1, keepdims=True)
        p = jnp.exp(s - m)
        den = jnp.sum(p, axis=-1, keepdims=True)
        o = jnp.dot(p.astype(BF16), vs, preferred_element_type=F32) * (1.0 / den)
        o_dst[sl(q_start, DIL_W), :] = o
        l_dst[sl(q_start, DIL_W), :] = jnp.broadcast_to(m + jnp.log(den), (DIL_W, LANES))

    block(0, 0, DIL_W, 1, o1, l1)

    def d1_body(n, _):
        block(pl.multiple_of(n * DIL_W, DIL_W), pl.multiple_of((n - 1) * DIL_W, DIL_W),
              2 * DIL_W, 1, o1, l1)
        return 0
    lax.fori_loop(1, seq // DIL_W, d1_body, 0)

    d = 4
    for r in range(d):
        block(r, r, DIL_W, d, o2, l2)

        def d4_body(n, _, r=r):
            block(r + n * (d * DIL_W), r + (n - 1) * (d * DIL_W), 2 * DIL_W, d, o2, l2)
            return 0
        lax.fori_loop(1, seq // (d * DIL_W), d4_body, 0)

    def d16_body(r, _):
        block(r, r, DIL_W, 16, o3, l3)
        return 0
    lax.fori_loop(0, 16, d16_body, 0)

    rows = 256

    def merge_body(t, _):
        rs = pl.ds(pl.multiple_of(t * rows, rows), rows)
        la, lb, lc = l1[rs, :], l2[rs, :], l3[rs, :]
        m = jnp.maximum(jnp.maximum(la, lb), lc)
        ea, eb, ec = jnp.exp(la - m), jnp.exp(lb - m), jnp.exp(lc - m)
        inv = 1.0 / (ea + eb + ec)
        out = (ea * inv) * o1[rs, :] + (eb * inv) * o2[rs, :] + (ec * inv) * o3[rs, :]
        o_ref[rs, :] = out.astype(o_ref.dtype)
        return 0
    lax.fori_loop(0, seq // rows, merge_body, 0)


def dilated_attention(u, tabs, batch, seq, n_heads):
    m = batch * seq
    blk = (seq, HEAD_DIM)
    tab_spec = pl.BlockSpec(blk, lambda b, h: (b, 0))
    scratch = [pltpu.VMEM(blk, F32) for _ in range(8)]
    return pl.pallas_call(
        functools.partial(_dilated_kernel, seq=seq),
        grid=(batch, n_heads),
        in_specs=[pl.BlockSpec(blk, lambda b, h: (b, h)),
                  pl.BlockSpec(blk, lambda b, h: (b, n_heads + h)),
                  pl.BlockSpec(blk, lambda b, h: (b, 2 * n_heads + h)),
                  tab_spec, tab_spec, tab_spec],
        out_specs=pl.BlockSpec(blk, lambda b, h: (b, h)),
        out_shape=jax.ShapeDtypeStruct((m, n_heads * HEAD_DIM), BF16),
        scratch_shapes=scratch,
        compiler_params=_cparams(("parallel", "parallel")),
        name="dilated_attention",
    )(u, u, u, *tabs)


def _convmod_kernel(a_ref, g_ref, ah_ref, gh_ref, cw_ref, cb_ref, lng_ref, lnb_ref, o_ref,
                    gs, cs, *, ts, ch):
    i = pl.program_id(1)
    halo = ah_ref[...] * jax.nn.sigmoid(gh_ref[...])
    gs[0:CONV_HALO, :] = jnp.where(i > 0, halo, 0.0)
    gs[CONV_HALO:, :] = a_ref[...] * jax.nn.sigmoid(g_ref[...])

    rb = 64
    first = CONV_HALO - (CONV_B_WIDTH - 1)
    for c in range(ch // LANES):
        cl = slice(c * LANES, (c + 1) * LANES)
        for r in range(ts // rb):
            acc = jnp.zeros((rb, LANES), F32)
            for k in range(CONV_B_WIDTH):
                acc = acc + cw_ref[k:k + 1, cl] * gs[r * rb + first + k:r * rb + first + k + rb, cl]
            cs[r * rb:(r + 1) * rb, cl] = acc + cb_ref[:, cl]

    lrows = 32

    def ln_body(t, _):
        rs = pl.ds(pl.multiple_of(t * lrows, lrows), lrows)
        x = cs[rs, :]
        mu = jnp.mean(x, axis=-1, keepdims=True)
        xc = x - mu
        var = jnp.mean(xc * xc, axis=-1, keepdims=True)
        y = xc * lax.rsqrt(var + EPS) * lng_ref[...] + lnb_ref[...]
        o_ref[rs, :] = (y * jax.nn.sigmoid(y)).astype(o_ref.dtype)
        return 0
    lax.fori_loop(0, ts // lrows, ln_body, 0)


def conv_module(u, col0, ch, conv_w, conv_b, ln_g, ln_b, batch, seq):
    m = batch * seq
    ts = 256
    nt = seq // ts
    hb = ts // CONV_HALO
    ca = col0 // ch
    cur = lambda cb: pl.BlockSpec((ts, ch), lambda b, i: (b * nt + i, cb))
    halo = lambda cb: pl.BlockSpec((CONV_HALO, ch),
                                   lambda b, i: (jnp.maximum((b * nt + i) * hb - 1, 0), cb))
    vec = lambda rows: pl.BlockSpec((rows, ch), lambda b, i: (0, 0))
    return pl.pallas_call(
        functools.partial(_convmod_kernel, ts=ts, ch=ch),
        grid=(batch, nt),
        in_specs=[cur(ca), cur(ca + 1), halo(ca), halo(ca + 1),
                  vec(CONV_B_WIDTH), vec(1), vec(1), vec(1)],
        out_specs=pl.BlockSpec((ts, ch), lambda b, i: (b * nt + i, 0)),
        out_shape=jax.ShapeDtypeStruct((m, ch), BF16),
        scratch_shapes=[pltpu.VMEM((CONV_HALO + ts, ch), F32), pltpu.VMEM((ts, ch), F32)],
        compiler_params=_cparams(("parallel", "parallel")),
        name="conv_module",
    )(u, u, u, u, conv_w, conv_b.reshape(1, ch), ln_g.reshape(1, ch), ln_b.reshape(1, ch))


def _gates_kernel(h_ref, w_ref, b_ref, f_ref, *, seq):
    fl = jnp.dot(h_ref[...], w_ref[...], preferred_element_type=F32) + b_ref[...]
    logf = jnp.minimum(fl, 0.0) - jnp.log1p(jnp.exp(-jnp.abs(fl)))
    blk = 128
    ii = lax.broadcasted_iota(jnp.int32, (blk, blk), 0)
    jj = lax.broadcasted_iota(jnp.int32, (blk, blk), 1)
    tri = jnp.where(jj <= ii, 1.0, 0.0).astype(BF16)
    carry = jnp.zeros((1, LANES), F32)
    for t in range(seq // blk):
        x = logf[t * blk:(t + 1) * blk, :]
        hi = x.astype(BF16)
        r1 = x - hi.astype(F32)
        mid = r1.astype(BF16)
        lo = (r1 - mid.astype(F32)).astype(BF16)
        cum = (jnp.dot(tri, hi, preferred_element_type=F32)
               + jnp.dot(tri, mid, preferred_element_type=F32)
               + jnp.dot(tri, lo, preferred_element_type=F32)) + carry
        f_ref[t * blk:(t + 1) * blk, :] = cum
        carry = cum[blk - 1:blk, :]


def forget_gates(h, w_f, b_f, batch, seq):
    m, d = h.shape
    return pl.pallas_call(
        functools.partial(_gates_kernel, seq=seq),
        grid=(batch,),
        in_specs=[pl.BlockSpec((seq, d), lambda b: (b, 0)),
                  pl.BlockSpec((d, LANES), lambda b: (0, 0)),
                  pl.BlockSpec((1, LANES), lambda b: (0, 0))],
        out_specs=pl.BlockSpec((seq, LANES), lambda b: (b, 0)),
        out_shape=jax.ShapeDtypeStruct((m, LANES), F32),
        compiler_params=_cparams(("parallel",)),
        name="forget_gates",
    )(h, w_f, b_f)


def _fox_kernel(q_ref, k_ref, v_ref, fn_ref, fr_ref, o_ref, *, tq):
    h = pl.program_id(1)
    qi = pl.program_id(2)
    q = q_ref[...]
    lane = lax.broadcasted_iota(jnp.int32, (tq, LANES), 1)
    fcol = jnp.sum(jnp.where(lane == h, fn_ref[...], 0.0), axis=1, keepdims=True)

    def scores(kj):
        ks = pl.ds(pl.multiple_of(kj * tq, tq), tq)
        s = lax.dot_general(q, k_ref[ks, :], _NT, preferred_element_type=F32)
        return s + (fcol - fr_ref[0, 0, pl.ds(kj, 1), :]), v_ref[ks, :]

    def update(carry, s, v):
        m, l, acc = carry
        m_new = jnp.maximum(m, jnp.max(s, axis=-1, keepdims=True))
        alpha = jnp.exp(m - m_new)
        p = jnp.exp(s - m_new)
        l = alpha * l + jnp.sum(p, axis=-1, keepdims=True)
        acc = alpha * acc + jnp.dot(p.astype(BF16), v, preferred_element_type=F32)
        return m_new, l, acc

    def body(kj, carry):
        s, v = scores(kj)
        return update(carry, s, v)

    init = (jnp.full((tq, 1), NEG, F32), jnp.zeros((tq, 1), F32), jnp.zeros((tq, HEAD_DIM), F32))
    carry = lax.fori_loop(0, qi, body, init)
    s, v = scores(qi)
    row = lax.broadcasted_iota(jnp.int32, s.shape, 0)
    col = lax.broadcasted_iota(jnp.int32, s.shape, 1)
    _, l, acc = update(carry, jnp.where(row >= col, s, NEG), v)
    o_ref[...] = (acc * (1.0 / l)).astype(o_ref.dtype)


def fox_attention(qkv, f_nat, f_row, batch, seq, n_heads, tq):
    m = batch * seq
    nq = seq // tq
    kv = lambda g: pl.BlockSpec((seq, HEAD_DIM), lambda b, h, i: (b, g * n_heads + h))
    return pl.pallas_call(
        functools.partial(_fox_kernel, tq=tq),
        grid=(batch, n_heads, nq),
        in_specs=[pl.BlockSpec((tq, HEAD_DIM), lambda b, h, i: (b * nq + i, h)),
                  kv(1), kv(2),
                  pl.BlockSpec((tq, LANES), lambda b, h, i: (b * nq + i, 0)),
                  pl.BlockSpec((1, 1, nq, tq), lambda b, h, i: (b, h, 0, 0))],
        out_specs=pl.BlockSpec((tq, HEAD_DIM), lambda b, h, i: (b * nq + i, h)),
        out_shape=jax.ShapeDtypeStruct((m, n_heads * HEAD_DIM), BF16),
        compiler_params=_cparams(("parallel", "parallel", "arbitrary")),
        name="fox_attention",
    )(qkv, qkv, qkv, f_nat, f_row)


def _ffn_up_kernel(h_ref, wg_ref, wu_ref, cw_ref, o_ref, gs, *, tm, tiles_per_seq):
    i = pl.program_id(1)

    @pl.when(i % tiles_per_seq == 0)
    def _():
        gs[0:SUBLANES, :] = jnp.zeros((SUBLANES, gs.shape[1]), F32)

    h = h_ref[...]
    gs[SUBLANES:, :] = jnp.dot(h, wg_ref[...], preferred_element_type=F32)
    u = jnp.dot(h, wu_ref[...], preferred_element_type=F32)
    conv = (cw_ref[0:1, :] * gs[SUBLANES - 2:SUBLANES - 2 + tm, :]
            + cw_ref[1:2, :] * gs[SUBLANES - 1:SUBLANES - 1 + tm, :]
            + cw_ref[2:3, :] * gs[SUBLANES:, :])
    o_ref[...] = (conv * jax.nn.sigmoid(conv) * u).astype(o_ref.dtype)
    gs[0:SUBLANES, :] = gs[tm:tm + SUBLANES, :]


def ffn_up(h, w_up, conv_w, d_ff, seq):
    m, d = h.shape
    tm, tn = 1024, 512
    nj = d_ff // tn
    return pl.pallas_call(
        functools.partial(_ffn_up_kernel, tm=tm, tiles_per_seq=seq // tm),
        grid=(nj, m // tm),
        in_specs=[pl.BlockSpec((tm, d), lambda j, i: (i, 0)),
                  pl.BlockSpec((d, tn), lambda j, i: (0, j)),
                  pl.BlockSpec((d, tn), lambda j, i: (0, nj + j)),
                  pl.BlockSpec((FFN_CONV_WIDTH, tn), lambda j, i: (0, j))],
        out_specs=pl.BlockSpec((tm, tn), lambda j, i: (i, j)),
        out_shape=jax.ShapeDtypeStruct((m, d_ff), BF16),
        scratch_shapes=[pltpu.VMEM((tm + SUBLANES, tn), F32)],
        compiler_params=_cparams(("parallel", "arbitrary")),
        name="ffn_up",
    )(h, w_up, w_up, conv_w)


def kernel(x, positions, norm_mix, norm_ffn, norm_final, ev_w_in, ev_conv_w, ev_conv_b, ev_ln_g,
           ev_ln_b, ev_w_out, od_w_in, od_b_f, od_w_out, ffn_w_up, ffn_conv_w, ffn_w_down):
    batch, seq, d = x.shape
    depth = norm_mix.shape[0]
    m = batch * seq
    c_b = ev_conv_w.shape[2]
    n_heads_a = (ev_w_in.shape[2] - 2 * c_b) // (3 * HEAD_DIM)
    n_heads_c = od_b_f.shape[1]
    d_a = n_heads_a * HEAD_DIM
    d_c = n_heads_c * HEAD_DIM
    d_ff = ffn_w_down.shape[1]
    fox_tq = 256

    xf = x.reshape(m, d)
    tabs = rope_tables(positions)

    for l in range(depth):
        h = rmsnorm(xf, norm_mix[l], BF16)
        if l % 2 == 0:
            e = l // 2
            u = matmul([h], ev_w_in[e].astype(BF16), ev_w_in.shape[2], F32, tm=1024, tn=1024)
            y_a = dilated_attention(u, tabs, batch, seq, n_heads_a)
            y_b = conv_module(u, 3 * d_a, c_b, ev_conv_w[e], ev_conv_b[e], ev_ln_g[e], ev_ln_b[e],
                              batch, seq)
            xf = matmul([y_a, y_b], ev_w_out[e].astype(BF16), d, F32, res=xf, tm=1024, tn=1024)
        else:
            o = l // 2
            w_in = od_w_in[o].astype(BF16)
            qkv = matmul([h], w_in, 3 * d_c, BF16, tm=1024, tn=1024,
                         scale=HEAD_DIM ** -0.5, scale_cols=d_c)
            w_f = jnp.pad(w_in[:, 3 * d_c:], ((0, 0), (0, LANES - n_heads_c)))
            b_f = jnp.pad(od_b_f[o], (0, LANES - n_heads_c)).reshape(1, LANES)
            f_nat = forget_gates(h, w_f, b_f, batch, seq)
            f_row = (f_nat.reshape(batch, seq, LANES)[:, :, :n_heads_c].transpose(0, 2, 1)
                     .reshape(batch, n_heads_c, seq // fox_tq, fox_tq))
            y = fox_attention(qkv, f_nat, f_row, batch, seq, n_heads_c, fox_tq)
            xf = matmul([y], od_w_out[o].astype(BF16), d, F32, res=xf, tm=1024, tn=1024)
        h = rmsnorm(xf, norm_ffn[l], BF16)
        act = ffn_up(h, ffn_w_up[l].astype(BF16), ffn_conv_w[l], d_ff, seq)
        xf = matmul([act], ffn_w_down[l].astype(BF16), d, F32, res=xf, tm=512, tn=512)
    return rmsnorm(xf, norm_final, F32).reshape(batch, seq, d)
```

```python
import functools

import jax
import jax.numpy as jnp
from jax import lax
from jax.experimental import pallas as pl
from jax.experimental.pallas import tpu as pltpu

F32 = jnp.float32
BF16 = jnp.bfloat16

HEAD_DIM = 128
ROT_DIM = 32
ROPE_THETA = 500000.0
CONV_B_WIDTH = 31
CONV_HALO = 32
FFN_CONV_WIDTH = 3
EPS = 1e-6
NEG = -1e30
LOG2E = 1.4426950408889634
DIL_W = 128
LANES = 128
SUBLANES = 8
VMEM_LIMIT = 56 * 1024 * 1024

_NT = (((1,), (1,)), ((), ()))
_TN = (((0,), (0,)), ((), ()))


def _cparams(sem):
    return pltpu.CompilerParams(dimension_semantics=sem, vmem_limit_bytes=VMEM_LIMIT)


def _rmsnorm_kernel(x_ref, g_ref, o_ref):
    x = x_ref[...]
    ms = jnp.mean(x * x, axis=-1, keepdims=True)
    o_ref[...] = (x * lax.rsqrt(ms + EPS) * g_ref[...]).astype(o_ref.dtype)


def rmsnorm(x, g, out_dtype):
    m, d = x.shape
    tm = 512
    return pl.pallas_call(
        _rmsnorm_kernel,
        grid=(m // tm,),
        in_specs=[pl.BlockSpec((tm, d), lambda i: (i, 0)),
                  pl.BlockSpec((1, d), lambda i: (0, 0))],
        out_specs=pl.BlockSpec((tm, d), lambda i: (i, 0)),
        out_shape=jax.ShapeDtypeStruct((m, d), out_dtype),
        compiler_params=_cparams(("parallel",)),
        name="rmsnorm",
    )(x, g.reshape(1, d))


def _mm_kernel(*refs, n_ops, has_res, scale, n_scale_tiles):
    a_refs = refs[:n_ops]
    w_refs = refs[n_ops:2 * n_ops]
    res_ref = refs[2 * n_ops] if has_res else None
    o_ref = refs[-1]
    acc = jnp.dot(a_refs[0][...], w_refs[0][...], preferred_element_type=F32)
    for a_ref, w_ref in zip(a_refs[1:], w_refs[1:]):
        acc = acc + jnp.dot(a_ref[...], w_ref[...], preferred_element_type=F32)
    if scale is not None:
        acc = acc * jnp.where(pl.program_id(0) < n_scale_tiles, scale, 1.0).astype(F32)
    if has_res:
        acc = acc + res_ref[...]
    o_ref[...] = acc.astype(o_ref.dtype)


def matmul(a_list, w, n_cols, out_dtype, *, res=None, tm, tn, scale=None, scale_cols=0):
    m = a_list[0].shape[0]
    kk = a_list[0].shape[1]
    n_ops = len(a_list)
    assert all(a.shape == (m, kk) for a in a_list) and w.shape[0] == n_ops * kk
    assert m % tm == 0 and n_cols % tn == 0 and scale_cols % tn == 0
    in_specs = [pl.BlockSpec((tm, kk), lambda j, i: (i, 0)) for _ in a_list]
    in_specs += [pl.BlockSpec((kk, tn), functools.partial(lambda j, i, r: (r, j), r=r))
                 for r in range(n_ops)]
    args = list(a_list) + [w] * n_ops
    if res is not None:
        in_specs.append(pl.BlockSpec((tm, tn), lambda j, i: (i, j)))
        args.append(res)
    kern = functools.partial(_mm_kernel, n_ops=n_ops, has_res=res is not None, scale=scale,
                             n_scale_tiles=scale_cols // tn)
    return pl.pallas_call(
        kern,
        grid=(n_cols // tn, m // tm),
        in_specs=in_specs,
        out_specs=pl.BlockSpec((tm, tn), lambda j, i: (i, j)),
        out_shape=jax.ShapeDtypeStruct((m, n_cols), out_dtype),
        compiler_params=_cparams(("parallel", "parallel")),
        name="matmul",
    )(*args)


def _rope_table_kernel(pos_ref, inv_ref, c_ref, sa_ref, sb_ref):
    ang = pos_ref[...].astype(F32) * inv_ref[...]
    lane = lax.broadcasted_iota(jnp.int32, ang.shape, 1)
    cos = jnp.cos(ang)
    sin = jnp.sin(ang)
    half = ROT_DIM // 2
    c_ref[...] = jnp.where(lane < ROT_DIM, cos, 1.0)
    sa_ref[...] = jnp.where(lane < half, -sin, 0.0)
    sb_ref[...] = jnp.where(lane < ROT_DIM, jnp.where(lane >= half, sin, 0.0), 0.0)


def rope_tables(positions):
    m = positions.size
    half = ROT_DIM // 2
    inv = jnp.power(jnp.float32(ROPE_THETA), -jnp.arange(half, dtype=F32) * (2.0 / ROT_DIM))
    inv_lane = jnp.concatenate([inv, inv, jnp.zeros((LANES - ROT_DIM,), F32)]).reshape(1, LANES)
    tm = 1024
    shp = jax.ShapeDtypeStruct((m, LANES), F32)
    spec = pl.BlockSpec((tm, LANES), lambda i: (i, 0))
    return pl.pallas_call(
        _rope_table_kernel,
        grid=(m // tm,),
        in_specs=[pl.BlockSpec((tm, 1), lambda i: (i, 0)),
                  pl.BlockSpec((1, LANES), lambda i: (0, 0))],
        out_specs=[spec, spec, spec],
        out_shape=[shp, shp, shp],
        compiler_params=_cparams(("parallel",)),
        name="rope_tables",
    )(positions.reshape(m, 1), inv_lane)


def _dilated_kernel(q_ref, k_ref, v_ref, c_ref, sa_ref, sb_ref, o_ref,
                    qr, kr, o1, o2, o3, l1, l2, l3, *, seq):
    half = ROT_DIM // 2
    c = c_ref[...]
    sa = sa_ref[...]
    sb = sb_ref[...]

    def rope(x):
        return x * c + pltpu.roll(x, LANES - half, 1) * sa + pltpu.roll(x, half, 1) * sb

    qr[...] = rope(q_ref[...]) * (HEAD_DIM ** -0.5)
    kr[...] = rope(k_ref[...])

    def block(q_start, k_start, nk, stride, o_dst, l_dst):
        def sl(start, n):
            return pl.ds(start, n) if stride == 1 else pl.ds(start, n, stride=stride)
        qs = qr[sl(q_start, DIL_W), :].astype(BF16)
        ks = kr[sl(k_start, nk), :].astype(BF16)
        vs = v_ref[sl(k_start, nk), :].astype(BF16)
        s = lax.dot_general(qs, ks, _NT, preferred_element_type=F32)
        i = lax.broadcasted_iota(jnp.int32, s.shape, 0)
        j = lax.broadcasted_iota(jnp.int32, s.shape, 1)
        dist = (nk - DIL_W) + i - j
        if nk == DIL_W:
            s = jnp.where(dist >= 0, s, NEG)
        else:
            s = jnp.where(dist >= 0, jnp.where(dist <= DIL_W, s, NEG), NEG)
        m = jnp.max(s, axis=-1, keepdims=True)
        p = jnp.exp(s - m)
        den = jnp.sum(p, axis=-1, keepdims=True)
        o = jnp.dot(p.astype(BF16), vs, preferred_element_type=F32) * (1.0 / den)
        o_dst[sl(q_start, DIL_W), :] = o
        l_dst[sl(q_start, DIL_W), :] = jnp.broadcast_to(m + jnp.log(den), (DIL_W, LANES))

    block(0, 0, DIL_W, 1, o1, l1)

    def d1_body(n, _):
        block(pl.multiple_of(n * DIL_W, DIL_W), pl.multiple_of((n - 1) * DIL_W, DIL_W),
              2 * DIL_W, 1, o1, l1)
        return 0
    lax.fori_loop(1, seq // DIL_W, d1_body, 0, unroll=5)

    d = 4
    for r in range(d):
        block(r, r, DIL_W, d, o2, l2)

        def d4_body(n, _, r=r):
            block(r + n * (d * DIL_W), r + (n - 1) * (d * DIL_W), 2 * DIL_W, d, o2, l2)
            return 0
        lax.fori_loop(1, seq // (d * DIL_W), d4_body, 0, unroll=True)

    def d16_body(r, _):
        block(r, r, DIL_W, 16, o3, l3)
        return 0
    lax.fori_loop(0, 16, d16_body, 0, unroll=4)

    rows = 256

    def merge_body(t, _):
        rs = pl.ds(pl.multiple_of(t * rows, rows), rows)
        la, lb, lc = l1[rs, :], l2[rs, :], l3[rs, :]
        m = jnp.maximum(jnp.maximum(la, lb), lc)
        ea, eb, ec = jnp.exp(la - m), jnp.exp(lb - m), jnp.exp(lc - m)
        inv = 1.0 / (ea + eb + ec)
        out = (ea * inv) * o1[rs, :] + (eb * inv) * o2[rs, :] + (ec * inv) * o3[rs, :]
        o_ref[rs, :] = out.astype(o_ref.dtype)
        return 0
    lax.fori_loop(0, seq // rows, merge_body, 0)


def dilated_attention(u, tabs, batch, seq, n_heads):
    m = batch * seq
    blk = (seq, HEAD_DIM)
    tab_spec = pl.BlockSpec(blk, lambda b, h: (b, 0))
    scratch = [pltpu.VMEM(blk, F32) for _ in range(8)]
    return pl.pallas_call(
        functools.partial(_dilated_kernel, seq=seq),
        grid=(batch, n_heads),
        in_specs=[pl.BlockSpec(blk, lambda b, h: (b, h)),
                  pl.BlockSpec(blk, lambda b, h: (b, n_heads + h)),
                  pl.BlockSpec(blk, lambda b, h: (b, 2 * n_heads + h)),
                  tab_spec, tab_spec, tab_spec],
        out_specs=pl.BlockSpec(blk, lambda b, h: (b, h)),
        out_shape=jax.ShapeDtypeStruct((m, n_heads * HEAD_DIM), BF16),
        scratch_shapes=scratch,
        compiler_params=_cparams(("parallel", "parallel")),
        name="dilated_attention",
    )(u, u, u, *tabs)


def _convmod_kernel(a_ref, g_ref, ah_ref, gh_ref, cw_ref, cb_ref, lng_ref, lnb_ref, o_ref,
                    gs, cs, *, ts, ch):
    i = pl.program_id(1)
    halo = ah_ref[...] * jax.nn.sigmoid(gh_ref[...])
    gs[0:CONV_HALO, :] = jnp.where(i > 0, halo, 0.0)
    gs[CONV_HALO:, :] = a_ref[...] * jax.nn.sigmoid(g_ref[...])

    rb = 64
    first = CONV_HALO - (CONV_B_WIDTH - 1)
    for c in range(ch // LANES):
        cl = slice(c * LANES, (c + 1) * LANES)
        for r in range(ts // rb):
            acc = jnp.zeros((rb, LANES), F32)
            for k in range(CONV_B_WIDTH):
                acc = acc + cw_ref[k:k + 1, cl] * gs[r * rb + first + k:r * rb + first + k + rb, cl]
            cs[r * rb:(r + 1) * rb, cl] = acc + cb_ref[:, cl]

    lrows = 32

    def ln_body(t, _):
        rs = pl.ds(pl.multiple_of(t * lrows, lrows), lrows)
        x = cs[rs, :]
        mu = jnp.mean(x, axis=-1, keepdims=True)
        xc = x - mu
        var = jnp.mean(xc * xc, axis=-1, keepdims=True)
        y = xc * lax.rsqrt(var + EPS) * lng_ref[...] + lnb_ref[...]
        o_ref[rs, :] = (y * jax.nn.sigmoid(y)).astype(o_ref.dtype)
        return 0
    lax.fori_loop(0, ts // lrows, ln_body, 0)


def conv_module(u, col0, ch, conv_w, conv_b, ln_g, ln_b, batch, seq):
    m = batch * seq
    ts = 256
    nt = seq // ts
    hb = ts // CONV_HALO
    ca = col0 // ch
    cur = lambda cb: pl.BlockSpec((ts, ch), lambda b, i: (b * nt + i, cb))
    halo = lambda cb: pl.BlockSpec((CONV_HALO, ch),
                                   lambda b, i: (jnp.maximum((b * nt + i) * hb - 1, 0), cb))
    vec = lambda rows: pl.BlockSpec((rows, ch), lambda b, i: (0, 0))
    return pl.pallas_call(
        functools.partial(_convmod_kernel, ts=ts, ch=ch),
        grid=(batch, nt),
        in_specs=[cur(ca), cur(ca + 1), halo(ca), halo(ca + 1),
                  vec(CONV_B_WIDTH), vec(1), vec(1), vec(1)],
        out_specs=pl.BlockSpec((ts, ch), lambda b, i: (b * nt + i, 0)),
        out_shape=jax.ShapeDtypeStruct((m, ch), BF16),
        scratch_shapes=[pltpu.VMEM((CONV_HALO + ts, ch), F32), pltpu.VMEM((ts, ch), F32)],
        compiler_params=_cparams(("parallel", "parallel")),
        name="conv_module",
    )(u, u, u, u, conv_w, conv_b.reshape(1, ch), ln_g.reshape(1, ch), ln_b.reshape(1, ch))


def _gates_kernel(h_ref, w_ref, b_ref, f_ref, *, seq):
    fl = jnp.dot(h_ref[...], w_ref[...], preferred_element_type=F32) + b_ref[...]
    logf = jnp.minimum(fl, 0.0) - jnp.log1p(jnp.exp(-jnp.abs(fl)))
    blk = 128
    ii = lax.broadcasted_iota(jnp.int32, (blk, blk), 0)
    jj = lax.broadcasted_iota(jnp.int32, (blk, blk), 1)
    tri = jnp.where(jj <= ii, 1.0, 0.0).astype(BF16)
    carry = jnp.zeros((1, LANES), F32)
    for t in range(seq // blk):
        x = logf[t * blk:(t + 1) * blk, :]
        hi = x.astype(BF16)
        r1 = x - hi.astype(F32)
        mid = r1.astype(BF16)
        lo = (r1 - mid.astype(F32)).astype(BF16)
        cum = (jnp.dot(tri, hi, preferred_element_type=F32)
               + jnp.dot(tri, mid, preferred_element_type=F32)
               + jnp.dot(tri, lo, preferred_element_type=F32)) + carry
        f_ref[t * blk:(t + 1) * blk, :] = cum
        carry = cum[blk - 1:blk, :]


def forget_gates(h, w_f, b_f, batch, seq):
    m, d = h.shape
    return pl.pallas_call(
        functools.partial(_gates_kernel, seq=seq),
        grid=(batch,),
        in_specs=[pl.BlockSpec((seq, d), lambda b: (b, 0)),
                  pl.BlockSpec((d, LANES), lambda b: (0, 0)),
                  pl.BlockSpec((1, LANES), lambda b: (0, 0))],
        out_specs=pl.BlockSpec((seq, LANES), lambda b: (b, 0)),
        out_shape=jax.ShapeDtypeStruct((m, LANES), F32),
        compiler_params=_cparams(("parallel",)),
        name="forget_gates",
    )(h, w_f, b_f)


def _fox_kernel(q_ref, k_ref, v_ref, fn_ref, o_ref, qa, ka, vt, s_scr, p_scr, *, seq, tq):
    h = pl.program_id(1)
    nblk = seq // tq
    lane = lax.broadcasted_iota(jnp.int32, (seq, LANES), 1)
    f2 = jnp.sum(jnp.where(lane == h, fn_ref[...], 0.0), axis=1, keepdims=True) * LOG2E
    base = [f2[b * tq:b * tq + 1, :] for b in range(nblk)]
    d = jnp.concatenate([f2[b * tq:(b + 1) * tq, :] - base[b] for b in range(nblk)], axis=0)
    hi = d.astype(BF16).astype(F32)
    r1 = d - hi
    mid = r1.astype(BF16).astype(F32)
    lo = (r1 - mid).astype(BF16).astype(F32)
    odd = (lane & 1) == 1
    term = jnp.where(lane < 2, hi, jnp.where(lane < 4, mid, lo))
    live = lane < 6
    aug_q = jnp.where(live, jnp.where(odd, 1.0, term), 0.0)
    aug_k = jnp.where(live, jnp.where(odd, -term, 1.0), 0.0)
    qa[:, :HEAD_DIM] = q_ref[...]
    qa[:, HEAD_DIM:] = aug_q.astype(BF16)
    ka[:, :HEAD_DIM] = k_ref[...]
    ka[:, HEAD_DIM:] = aug_k.astype(BF16)
    vt[...] = v_ref[...].T

    row = lax.broadcasted_iota(jnp.int32, (tq, tq), 0)
    col = lax.broadcasted_iota(jnp.int32, (tq, tq), 1)
    pairs = [(qi, kj) for qi in range(nblk) for kj in range(qi + 1)]

    def scores(qi, kj):
        return lax.dot_general(ka[kj * tq:(kj + 1) * tq, :], qa[qi * tq:(qi + 1) * tq, :], _NT,
                               preferred_element_type=F32)

    def softmax(t, m, l):
        qi, kj = pairs[t]
        s = s_scr[t % 2]
        if kj == qi:
            s = jnp.where(row <= col, s, NEG)
        blk_max = jnp.max(s, axis=0, keepdims=True)
        if kj != qi:
            blk_max = blk_max + (base[qi] - base[kj])
        if kj == 0:
            m_new = blk_max
            alpha = None
        else:
            m_new = jnp.maximum(m, blk_max)
            alpha = jnp.exp2(m - m_new)
        p = jnp.exp2(s - (m_new if kj == qi else m_new - (base[qi] - base[kj])))
        p_scr[t % 2] = p.astype(BF16)
        psum = jnp.sum(p, axis=0, keepdims=True)
        return m_new, (psum if kj == 0 else alpha * l + psum), alpha

    n = len(pairs)
    s_scr[0] = scores(*pairs[0])
    if n > 1:
        s_scr[1] = scores(*pairs[1])
    m, l, alpha = softmax(0, None, None)
    acc = None
    for t, (qi, kj) in enumerate(pairs):
        if t + 2 < n:
            s_scr[t % 2] = scores(*pairs[t + 2])
        pv = jnp.dot(vt[:, kj * tq:(kj + 1) * tq], p_scr[t % 2],
                     preferred_element_type=F32)
        acc = pv if kj == 0 else alpha * acc + pv
        l_t = l
        if t + 1 < n:
            m, l, alpha = softmax(t + 1, m, l)
        if kj == qi:
            o_ref[qi * tq:(qi + 1) * tq, :] = (acc * (1.0 / l_t)).T.astype(o_ref.dtype)


def fox_attention(qkv, f_nat, batch, seq, n_heads):
    m = batch * seq
    tq = 256
    blk = lambda g: pl.BlockSpec((seq, HEAD_DIM), lambda b, h: (b, g * n_heads + h))
    return pl.pallas_call(
        functools.partial(_fox_kernel, seq=seq, tq=tq),
        grid=(batch, n_heads),
        in_specs=[blk(0), blk(1), blk(2), pl.BlockSpec((seq, LANES), lambda b, h: (b, 0))],
        out_specs=pl.BlockSpec((seq, HEAD_DIM), lambda b, h: (b, h)),
        out_shape=jax.ShapeDtypeStruct((m, n_heads * HEAD_DIM), BF16),
        scratch_shapes=[pltpu.VMEM((seq, 2 * HEAD_DIM), BF16), pltpu.VMEM((seq, 2 * HEAD_DIM), BF16),
                        pltpu.VMEM((HEAD_DIM, seq), BF16),
                        pltpu.VMEM((2, tq, tq), F32), pltpu.VMEM((2, tq, tq), BF16)],
        compiler_params=_cparams(("parallel", "parallel")),
        name="fox_attention",
    )(qkv, qkv, qkv, f_nat)


def _ffn_up_kernel(h_ref, wg_ref, wu_ref, cw_ref, o_ref, gs, *, tm, tiles_per_seq):
    i = pl.program_id(1)

    @pl.when(i % tiles_per_seq == 0)
    def _():
        gs[0:SUBLANES, :] = jnp.zeros((SUBLANES, gs.shape[1]), F32)

    h = h_ref[...]
    gs[SUBLANES:, :] = jnp.dot(h, wg_ref[...], preferred_element_type=F32)
    u = jnp.dot(h, wu_ref[...], preferred_element_type=F32)
    conv = (cw_ref[0:1, :] * gs[SUBLANES - 2:SUBLANES - 2 + tm, :]
            + cw_ref[1:2, :] * gs[SUBLANES - 1:SUBLANES - 1 + tm, :]
            + cw_ref[2:3, :] * gs[SUBLANES:, :])
    o_ref[...] = (conv * jax.nn.sigmoid(conv) * u).astype(o_ref.dtype)
    gs[0:SUBLANES, :] = gs[tm:tm + SUBLANES, :]


def ffn_up(h, w_up, conv_w, d_ff, seq):
    m, d = h.shape
    tm, tn = 1024, 512
    nj = d_ff // tn
    return pl.pallas_call(
        functools.partial(_ffn_up_kernel, tm=tm, tiles_per_seq=seq // tm),
        grid=(nj, m // tm),
        in_specs=[pl.BlockSpec((tm, d), lambda j, i: (i, 0)),
                  pl.BlockSpec((d, tn), lambda j, i: (0, j)),
                  pl.BlockSpec((d, tn), lambda j, i: (0, nj + j)),
                  pl.BlockSpec((FFN_CONV_WIDTH, tn), lambda j, i: (0, j))],
        out_specs=pl.BlockSpec((tm, tn), lambda j, i: (i, j)),
        out_shape=jax.ShapeDtypeStruct((m, d_ff), BF16),
        scratch_shapes=[pltpu.VMEM((tm + SUBLANES, tn), F32)],
        compiler_params=_cparams(("parallel", "arbitrary")),
        name="ffn_up",
    )(h, w_up, w_up, conv_w)


def kernel(x, positions, norm_mix, norm_ffn, norm_final, ev_w_in, ev_conv_w, ev_conv_b, ev_ln_g,
           ev_ln_b, ev_w_out, od_w_in, od_b_f, od_w_out, ffn_w_up, ffn_conv_w, ffn_w_down):
    batch, seq, d = x.shape
    depth = norm_mix.shape[0]
    m = batch * seq
    c_b = ev_conv_w.shape[2]
    n_heads_a = (ev_w_in.shape[2] - 2 * c_b) // (3 * HEAD_DIM)
    n_heads_c = od_b_f.shape[1]
    d_a = n_heads_a * HEAD_DIM
    d_c = n_heads_c * HEAD_DIM
    d_ff = ffn_w_down.shape[1]
    fox_tq = 256

    xf = x.reshape(m, d)
    tabs = rope_tables(positions)

    for l in range(depth):
        h = rmsnorm(xf, norm_mix[l], BF16)
        if l % 2 == 0:
            e = l // 2
            u = matmul([h], ev_w_in[e].astype(BF16), ev_w_in.shape[2], F32, tm=1024, tn=1024)
            y_a = dilated_attention(u, tabs, batch, seq, n_heads_a)
            y_b = conv_module(u, 3 * d_a, c_b, ev_conv_w[e], ev_conv_b[e], ev_ln_g[e], ev_ln_b[e],
                              batch, seq)
            xf = matmul([y_a, y_b], ev_w_out[e].astype(BF16), d, F32, res=xf, tm=1024, tn=1024)
        else:
            o = l // 2
            w_in = od_w_in[o].astype(BF16)
            qkv = matmul([h], w_in, 3 * d_c, BF16, tm=1024, tn=1024,
                         scale=LOG2E * HEAD_DIM ** -0.5, scale_cols=d_c)
            w_f = jnp.pad(w_in[:, 3 * d_c:], ((0, 0), (0, LANES - n_heads_c)))
            b_f = jnp.pad(od_b_f[o], (0, LANES - n_heads_c)).reshape(1, LANES)
            f_nat = forget_gates(h, w_f, b_f, batch, seq)
            y = fox_attention(qkv, f_nat, batch, seq, n_heads_c)
            xf = matmul([y], od_w_out[o].astype(BF16), d, F32, res=xf, tm=1024, tn=1024)
        h = rmsnorm(xf, norm_ffn[l], BF16)
        act = ffn_up(h, ffn_w_up[l].astype(BF16), ffn_conv_w[l], d_ff, seq)
        xf = matmul([act], ffn_w_down[l].astype(BF16), d, F32, res=xf, tm=512, tn=512)
    return rmsnorm(xf, norm_final, F32).reshape(batch, seq, d)
```

```python
import functools

import jax
import jax.numpy as jnp
from jax import lax
from jax.experimental import pallas as pl
from jax.experimental.pallas import tpu as pltpu

F32 = jnp.float32
BF16 = jnp.bfloat16

HEAD_DIM = 128
ROT_DIM = 32
ROPE_THETA = 500000.0
CONV_B_WIDTH = 31
CONV_HALO = 32
FFN_CONV_WIDTH = 3
EPS = 1e-6
NEG = -1e30
LOG2E = 1.4426950408889634
DIL_W = 128
LANES = 128
SUBLANES = 8
VMEM_LIMIT = 56 * 1024 * 1024

_NT = (((1,), (1,)), ((), ()))
_TN = (((0,), (0,)), ((), ()))


def _cparams(sem):
    return pltpu.CompilerParams(dimension_semantics=sem, vmem_limit_bytes=VMEM_LIMIT)


def _rmsnorm_kernel(x_ref, g_ref, o_ref):
    x = x_ref[...]
    ms = jnp.mean(x * x, axis=-1, keepdims=True)
    o_ref[...] = (x * lax.rsqrt(ms + EPS) * g_ref[...]).astype(o_ref.dtype)


def rmsnorm(x, g, out_dtype):
    m, d = x.shape
    tm = 512
    return pl.pallas_call(
        _rmsnorm_kernel,
        grid=(m // tm,),
        in_specs=[pl.BlockSpec((tm, d), lambda i: (i, 0)),
                  pl.BlockSpec((1, d), lambda i: (0, 0))],
        out_specs=pl.BlockSpec((tm, d), lambda i: (i, 0)),
        out_shape=jax.ShapeDtypeStruct((m, d), out_dtype),
        compiler_params=_cparams(("parallel",)),
        name="rmsnorm",
    )(x, g.reshape(1, d))


def _mm_kernel(*refs, n_ops, has_res, scale, n_scale_tiles):
    a_refs = refs[:n_ops]
    w_refs = refs[n_ops:2 * n_ops]
    res_ref = refs[2 * n_ops] if has_res else None
    o_ref = refs[2 * n_ops + has_res]
    wb_refs = refs[2 * n_ops + has_res + 1:]

    @pl.when(pl.program_id(1) == 0)
    def _():
        for w_ref, wb_ref in zip(w_refs, wb_refs):
            wb_ref[...] = w_ref[...].astype(BF16)

    acc = jnp.dot(a_refs[0][...], wb_refs[0][...], preferred_element_type=F32)
    for a_ref, wb_ref in zip(a_refs[1:], wb_refs[1:]):
        acc = acc + jnp.dot(a_ref[...], wb_ref[...], preferred_element_type=F32)
    if scale is not None:
        acc = acc * jnp.where(pl.program_id(0) < n_scale_tiles, scale, 1.0).astype(F32)
    if has_res:
        acc = acc + res_ref[...]
    o_ref[...] = acc.astype(o_ref.dtype)


def matmul(a_list, w, layer, n_cols, out_dtype, *, res=None, tm, tn, scale=None, scale_cols=0):
    m = a_list[0].shape[0]
    kk = a_list[0].shape[1]
    n_ops = len(a_list)
    assert all(a.shape == (m, kk) for a in a_list) and w.shape[1] == n_ops * kk
    assert m % tm == 0 and n_cols % tn == 0 and scale_cols % tn == 0
    in_specs = [pl.BlockSpec((tm, kk), lambda j, i: (i, 0)) for _ in a_list]
    in_specs += [pl.BlockSpec((None, kk, tn), functools.partial(lambda j, i, r: (layer, r, j), r=r))
                 for r in range(n_ops)]
    args = list(a_list) + [w] * n_ops
    if res is not None:
        in_specs.append(pl.BlockSpec((tm, tn), lambda j, i: (i, j)))
        args.append(res)
    kern = functools.partial(_mm_kernel, n_ops=n_ops, has_res=res is not None, scale=scale,
                             n_scale_tiles=scale_cols // tn)
    return pl.pallas_call(
        kern,
        grid=(n_cols // tn, m // tm),
        in_specs=in_specs,
        out_specs=pl.BlockSpec((tm, tn), lambda j, i: (i, j)),
        out_shape=jax.ShapeDtypeStruct((m, n_cols), out_dtype),
        scratch_shapes=[pltpu.VMEM((kk, tn), BF16) for _ in range(n_ops)],
        compiler_params=_cparams(("parallel", "arbitrary")),
        name="matmul",
    )(*args)


def _rope_table_kernel(pos_ref, inv_ref, c_ref, sa_ref, sb_ref):
    ang = pos_ref[...].astype(F32) * inv_ref[...]
    lane = lax.broadcasted_iota(jnp.int32, ang.shape, 1)
    cos = jnp.cos(ang)
    sin = jnp.sin(ang)
    half = ROT_DIM // 2
    c_ref[...] = jnp.where(lane < ROT_DIM, cos, 1.0)
    sa_ref[...] = jnp.where(lane < half, -sin, 0.0)
    sb_ref[...] = jnp.where(lane < ROT_DIM, jnp.where(lane >= half, sin, 0.0), 0.0)


def rope_tables(positions):
    m = positions.size
    half = ROT_DIM // 2
    inv = jnp.power(jnp.float32(ROPE_THETA), -jnp.arange(half, dtype=F32) * (2.0 / ROT_DIM))
    inv_lane = jnp.concatenate([inv, inv, jnp.zeros((LANES - ROT_DIM,), F32)]).reshape(1, LANES)
    tm = 1024
    shp = jax.ShapeDtypeStruct((m, LANES), F32)
    spec = pl.BlockSpec((tm, LANES), lambda i: (i, 0))
    return pl.pallas_call(
        _rope_table_kernel,
        grid=(m // tm,),
        in_specs=[pl.BlockSpec((tm, 1), lambda i: (i, 0)),
                  pl.BlockSpec((1, LANES), lambda i: (0, 0))],
        out_specs=[spec, spec, spec],
        out_shape=[shp, shp, shp],
        compiler_params=_cparams(("parallel",)),
        name="rope_tables",
    )(positions.reshape(m, 1), inv_lane)


def _dilated_kernel(q_ref, k_ref, v_ref, c_ref, sa_ref, sb_ref, o_ref,
                    qr, kr, o1, o2, o3, l1, l2, l3, s_scr, p_scr, *, seq):
    half = ROT_DIM // 2
    c = c_ref[...]
    sa = sa_ref[...]
    sb = sb_ref[...]

    def rope(x):
        return x * c + pltpu.roll(x, LANES - half, 1) * sa + pltpu.roll(x, half, 1) * sb

    qr[...] = rope(q_ref[...]) * (HEAD_DIM ** -0.5)
    kr[...] = rope(k_ref[...])

    blocks = []
    for dst, dil in ((0, 1), (1, 4), (2, 16)):
        for r in range(dil):
            for n in range(seq // (dil * DIL_W)):
                q0 = r + n * dil * DIL_W
                blocks.append((dst, q0, q0 - dil * DIL_W, 2 * DIL_W, dil) if n else
                              (dst, q0, q0, DIL_W, dil))
    outs = ((o1, l1), (o2, l2), (o3, l3))

    def sl(start, n, stride):
        return pl.ds(start, n) if stride == 1 else pl.ds(start, n, stride=stride)

    i = lax.broadcasted_iota(jnp.int32, (DIL_W, 2 * DIL_W), 0)
    j = lax.broadcasted_iota(jnp.int32, (DIL_W, 2 * DIL_W), 1)
    dist = DIL_W + i - j
    band2 = (dist >= 0) & (dist <= DIL_W)
    band1 = (lax.broadcasted_iota(jnp.int32, (DIL_W, DIL_W), 0)
             >= lax.broadcasted_iota(jnp.int32, (DIL_W, DIL_W), 1))

    def scores(t):
        _, q0, k0, nk, st = blocks[t]
        qs = qr[sl(q0, DIL_W, st), :].astype(BF16)
        ks = kr[sl(k0, nk, st), :].astype(BF16)
        s_scr[t % 2, :, :nk] = lax.dot_general(qs, ks, _NT, preferred_element_type=F32)

    def softmax(t):
        nk = blocks[t][3]
        s = jnp.where(band1 if nk == DIL_W else band2, s_scr[t % 2, :, :nk], NEG)
        m = jnp.max(s, axis=-1, keepdims=True)
        p = jnp.exp(s - m)
        p_scr[t % 2, :, :nk] = p.astype(BF16)
        return m, jnp.sum(p, axis=-1, keepdims=True)

    def finish(t, m, den):
        dst, q0, k0, nk, st = blocks[t]
        vs = v_ref[sl(k0, nk, st), :].astype(BF16)
        o = jnp.dot(p_scr[t % 2, :, :nk], vs, preferred_element_type=F32) * (1.0 / den)
        o_dst, l_dst = outs[dst]
        o_dst[sl(q0, DIL_W, st), :] = o
        l_dst[sl(q0, DIL_W, st), :] = jnp.broadcast_to(m + jnp.log(den), (DIL_W, LANES))

    nblocks = len(blocks)
    scores(0)
    scores(1)
    stats = softmax(0)
    for t in range(nblocks):
        if t + 2 < nblocks:
            scores(t + 2)
        finish(t, *stats)
        if t + 1 < nblocks:
            stats = softmax(t + 1)

    rows = 256

    def merge_body(t, _):
        rs = pl.ds(pl.multiple_of(t * rows, rows), rows)
        la, lb, lc = l1[rs, :], l2[rs, :], l3[rs, :]
        m = jnp.maximum(jnp.maximum(la, lb), lc)
        ea, eb, ec = jnp.exp(la - m), jnp.exp(lb - m), jnp.exp(lc - m)
        inv = 1.0 / (ea + eb + ec)
        out = (ea * inv) * o1[rs, :] + (eb * inv) * o2[rs, :] + (ec * inv) * o3[rs, :]
        o_ref[rs, :] = out.astype(o_ref.dtype)
        return 0
    lax.fori_loop(0, seq // rows, merge_body, 0)


def dilated_attention(u, tabs, batch, seq, n_heads):
    m = batch * seq
    blk = (seq, HEAD_DIM)
    tab_spec = pl.BlockSpec(blk, lambda b, h: (b, 0))
    scratch = [pltpu.VMEM(blk, F32) for _ in range(8)]
    scratch += [pltpu.VMEM((2, DIL_W, 2 * DIL_W), F32), pltpu.VMEM((2, DIL_W, 2 * DIL_W), BF16)]
    return pl.pallas_call(
        functools.partial(_dilated_kernel, seq=seq),
        grid=(batch, n_heads),
        in_specs=[pl.BlockSpec(blk, lambda b, h: (b, h)),
                  pl.BlockSpec(blk, lambda b, h: (b, n_heads + h)),
                  pl.BlockSpec(blk, lambda b, h: (b, 2 * n_heads + h)),
                  tab_spec, tab_spec, tab_spec],
        out_specs=pl.BlockSpec(blk, lambda b, h: (b, h)),
        out_shape=jax.ShapeDtypeStruct((m, n_heads * HEAD_DIM), BF16),
        scratch_shapes=scratch,
        compiler_params=_cparams(("parallel", "parallel")),
        name="dilated_attention",
    )(u, u, u, *tabs)


def _convmod_kernel(a_ref, g_ref, ah_ref, gh_ref, cw_ref, cb_ref, lng_ref, lnb_ref, o_ref,
                    gs, cs, *, ts, ch):
    i = pl.program_id(1)
    halo = ah_ref[...] * jax.nn.sigmoid(gh_ref[...])
    gs[0:CONV_HALO, :] = jnp.where(i > 0, halo, 0.0)
    gs[CONV_HALO:, :] = a_ref[...] * jax.nn.sigmoid(g_ref[...])

    rb = 64
    first = CONV_HALO - (CONV_B_WIDTH - 1)
    for c in range(ch // LANES):
        cl = slice(c * LANES, (c + 1) * LANES)
        for r in range(ts // rb):
            acc = jnp.zeros((rb, LANES), F32)
            for k in range(CONV_B_WIDTH):
                acc = acc + cw_ref[k:k + 1, cl] * gs[r * rb + first + k:r * rb + first + k + rb, cl]
            cs[r * rb:(r + 1) * rb, cl] = acc + cb_ref[:, cl]

    lrows = 32

    def ln_body(t, _):
        rs = pl.ds(pl.multiple_of(t * lrows, lrows), lrows)
        x = cs[rs, :]
        mu = jnp.mean(x, axis=-1, keepdims=True)
        xc = x - mu
        var = jnp.mean(xc * xc, axis=-1, keepdims=True)
        y = xc * lax.rsqrt(var + EPS) * lng_ref[...] + lnb_ref[...]
        o_ref[rs, :] = (y * jax.nn.sigmoid(y)).astype(o_ref.dtype)
        return 0
    lax.fori_loop(0, ts // lrows, ln_body, 0)


def conv_module(u, col0, ch, conv_w, conv_b, ln_g, ln_b, batch, seq):
    m = batch * seq
    ts = 256
    nt = seq // ts
    hb = ts // CONV_HALO
    ca = col0 // ch
    cur = lambda cb: pl.BlockSpec((ts, ch), lambda b, i: (b * nt + i, cb))
    halo = lambda cb: pl.BlockSpec((CONV_HALO, ch),
                                   lambda b, i: (jnp.maximum((b * nt + i) * hb - 1, 0), cb))
    vec = lambda rows: pl.BlockSpec((rows, ch), lambda b, i: (0, 0))
    return pl.pallas_call(
        functools.partial(_convmod_kernel, ts=ts, ch=ch),
        grid=(batch, nt),
        in_specs=[cur(ca), cur(ca + 1), halo(ca), halo(ca + 1),
                  vec(CONV_B_WIDTH), vec(1), vec(1), vec(1)],
        out_specs=pl.BlockSpec((ts, ch), lambda b, i: (b * nt + i, 0)),
        out_shape=jax.ShapeDtypeStruct((m, ch), BF16),
        scratch_shapes=[pltpu.VMEM((CONV_HALO + ts, ch), F32), pltpu.VMEM((ts, ch), F32)],
        compiler_params=_cparams(("parallel", "parallel")),
        name="conv_module",
    )(u, u, u, u, conv_w, conv_b.reshape(1, ch), ln_g.reshape(1, ch), ln_b.reshape(1, ch))


def _gates_kernel(h_ref, w_ref, b_ref, f_ref, *, seq):
    fl = jnp.dot(h_ref[...], w_ref[...].astype(BF16), preferred_element_type=F32) + b_ref[...]
    logf = jnp.minimum(fl, 0.0) - jnp.log1p(jnp.exp(-jnp.abs(fl)))
    blk = 128
    ii = lax.broadcasted_iota(jnp.int32, (blk, blk), 0)
    jj = lax.broadcasted_iota(jnp.int32, (blk, blk), 1)
    tri = jnp.where(jj <= ii, 1.0, 0.0).astype(BF16)
    carry = jnp.zeros((1, LANES), F32)
    for t in range(seq // blk):
        x = logf[t * blk:(t + 1) * blk, :]
        hi = x.astype(BF16)
        r1 = x - hi.astype(F32)
        mid = r1.astype(BF16)
        lo = (r1 - mid.astype(F32)).astype(BF16)
        cum = (jnp.dot(tri, hi, preferred_element_type=F32)
               + jnp.dot(tri, mid, preferred_element_type=F32)
               + jnp.dot(tri, lo, preferred_element_type=F32)) + carry
        f_ref[t * blk:(t + 1) * blk, :] = cum
        carry = cum[blk - 1:blk, :]


def forget_gates(h, w_f, b_f, batch, seq):
    m, d = h.shape
    return pl.pallas_call(
        functools.partial(_gates_kernel, seq=seq),
        grid=(batch,),
        in_specs=[pl.BlockSpec((seq, d), lambda b: (b, 0)),
                  pl.BlockSpec((d, LANES), lambda b: (0, 0)),
                  pl.BlockSpec((1, LANES), lambda b: (0, 0))],
        out_specs=pl.BlockSpec((seq, LANES), lambda b: (b, 0)),
        out_shape=jax.ShapeDtypeStruct((m, LANES), F32),
        compiler_params=_cparams(("parallel",)),
        name="forget_gates",
    )(h, w_f, b_f)


def _fox_kernel(q_ref, k_ref, v_ref, fn_ref, o_ref, qa, ka, vt, s_scr, p_scr, *, seq, tq):
    h = pl.program_id(1)
    nblk = seq // tq
    lane = lax.broadcasted_iota(jnp.int32, (seq, LANES), 1)
    f2 = jnp.sum(jnp.where(lane == h, fn_ref[...], 0.0), axis=1, keepdims=True) * LOG2E
    base = [f2[b * tq:b * tq + 1, :] for b in range(nblk)]
    d = jnp.concatenate([f2[b * tq:(b + 1) * tq, :] - base[b] for b in range(nblk)], axis=0)
    hi = d.astype(BF16).astype(F32)
    r1 = d - hi
    mid = r1.astype(BF16).astype(F32)
    lo = (r1 - mid).astype(BF16).astype(F32)
    odd = (lane & 1) == 1
    term = jnp.where(lane < 2, hi, jnp.where(lane < 4, mid, lo))
    live = lane < 6
    aug_q = jnp.where(live, jnp.where(odd, 1.0, term), 0.0)
    aug_k = jnp.where(live, jnp.where(odd, -term, 1.0), 0.0)
    qa[:, :HEAD_DIM] = q_ref[...]
    qa[:, HEAD_DIM:] = aug_q.astype(BF16)
    ka[:, :HEAD_DIM] = k_ref[...]
    ka[:, HEAD_DIM:] = aug_k.astype(BF16)
    vt[...] = v_ref[...].T

    row = lax.broadcasted_iota(jnp.int32, (tq, tq), 0)
    col = lax.broadcasted_iota(jnp.int32, (tq, tq), 1)
    pairs = [(qi, kj) for qi in range(nblk) for kj in range(qi + 1)]

    def scores(qi, kj):
        return lax.dot_general(ka[kj * tq:(kj + 1) * tq, :], qa[qi * tq:(qi + 1) * tq, :], _NT,
                               preferred_element_type=F32)

    def softmax(t, m, l):
        qi, kj = pairs[t]
        s = s_scr[t % 2]
        if kj == qi:
            s = jnp.where(row <= col, s, NEG)
        blk_max = jnp.max(s, axis=0, keepdims=True)
        if kj != qi:
            blk_max = blk_max + (base[qi] - base[kj])
        if kj == 0:
            m_new = blk_max
            alpha = None
        else:
            m_new = jnp.maximum(m, blk_max)
            alpha = jnp.exp2(m - m_new)
        p = jnp.exp2(s - (m_new if kj == qi else m_new - (base[qi] - base[kj])))
        p_scr[t % 2] = p.astype(BF16)
        psum = jnp.sum(p, axis=0, keepdims=True)
        return m_new, (psum if kj == 0 else alpha * l + psum), alpha

    n = len(pairs)
    s_scr[0] = scores(*pairs[0])
    if n > 1:
        s_scr[1] = scores(*pairs[1])
    m, l, alpha = softmax(0, None, None)
    acc = None
    for t, (qi, kj) in enumerate(pairs):
        if t + 2 < n:
            s_scr[t % 2] = scores(*pairs[t + 2])
        pv = jnp.dot(vt[:, kj * tq:(kj + 1) * tq], p_scr[t % 2],
                     preferred_element_type=F32)
        acc = pv if kj == 0 else alpha * acc + pv
        l_t = l
        if t + 1 < n:
            m, l, alpha = softmax(t + 1, m, l)
        if kj == qi:
            o_ref[qi * tq:(qi + 1) * tq, :] = (acc * (1.0 / l_t)).T.astype(o_ref.dtype)


def fox_attention(qkv, f_nat, batch, seq, n_heads):
    m = batch * seq
    tq = 256
    blk = lambda g: pl.BlockSpec((seq, HEAD_DIM), lambda b, h: (b, g * n_heads + h))
    return pl.pallas_call(
        functools.partial(_fox_kernel, seq=seq, tq=tq),
        grid=(batch, n_heads),
        in_specs=[blk(0), blk(1), blk(2), pl.BlockSpec((seq, LANES), lambda b, h: (b, 0))],
        out_specs=pl.BlockSpec((seq, HEAD_DIM), lambda b, h: (b, h)),
        out_shape=jax.ShapeDtypeStruct((m, n_heads * HEAD_DIM), BF16),
        scratch_shapes=[pltpu.VMEM((seq, 2 * HEAD_DIM), BF16), pltpu.VMEM((seq, 2 * HEAD_DIM), BF16),
                        pltpu.VMEM((HEAD_DIM, seq), BF16),
                        pltpu.VMEM((2, tq, tq), F32), pltpu.VMEM((2, tq, tq), BF16)],
        compiler_params=_cparams(("parallel", "parallel")),
        name="fox_attention",
    )(qkv, qkv, qkv, f_nat)


def _ffn_up_kernel(h_ref, wg_ref, wu_ref, cw_ref, o_ref, gs, wgb, wub, *, tm, tiles_per_seq):
    i = pl.program_id(1)

    @pl.when(i == 0)
    def _():
        wgb[...] = wg_ref[...].astype(BF16)
        wub[...] = wu_ref[...].astype(BF16)

    @pl.when(i % tiles_per_seq == 0)
    def _():
        gs[0:SUBLANES, :] = jnp.zeros((SUBLANES, gs.shape[1]), F32)

    h = h_ref[...]
    gs[SUBLANES:, :] = jnp.dot(h, wgb[...], preferred_element_type=F32)
    u = jnp.dot(h, wub[...], preferred_element_type=F32)
    conv = (cw_ref[0:1, :] * gs[SUBLANES - 2:SUBLANES - 2 + tm, :]
            + cw_ref[1:2, :] * gs[SUBLANES - 1:SUBLANES - 1 + tm, :]
            + cw_ref[2:3, :] * gs[SUBLANES:, :])
    o_ref[...] = (conv * jax.nn.sigmoid(conv) * u).astype(o_ref.dtype)
    gs[0:SUBLANES, :] = gs[tm:tm + SUBLANES, :]


def ffn_up(h, w_up, conv_w, layer, d_ff, seq):
    m, d = h.shape
    tm, tn = 1024, 512
    nj = d_ff // tn
    return pl.pallas_call(
        functools.partial(_ffn_up_kernel, tm=tm, tiles_per_seq=seq // tm),
        grid=(nj, m // tm),
        in_specs=[pl.BlockSpec((tm, d), lambda j, i: (i, 0)),
                  pl.BlockSpec((None, d, tn), lambda j, i: (layer, 0, j)),
                  pl.BlockSpec((None, d, tn), lambda j, i: (layer, 0, nj + j)),
                  pl.BlockSpec((None, FFN_CONV_WIDTH, tn), lambda j, i: (layer, 0, j))],
        out_specs=pl.BlockSpec((tm, tn), lambda j, i: (i, j)),
        out_shape=jax.ShapeDtypeStruct((m, d_ff), BF16),
        scratch_shapes=[pltpu.VMEM((tm + SUBLANES, tn), F32),
                        pltpu.VMEM((d, tn), BF16), pltpu.VMEM((d, tn), BF16)],
        compiler_params=_cparams(("parallel", "arbitrary")),
        name="ffn_up",
    )(h, w_up, w_up, conv_w)


def kernel(x, positions, norm_mix, norm_ffn, norm_final, ev_w_in, ev_conv_w, ev_conv_b, ev_ln_g,
           ev_ln_b, ev_w_out, od_w_in, od_b_f, od_w_out, ffn_w_up, ffn_conv_w, ffn_w_down):
    batch, seq, d = x.shape
    depth = norm_mix.shape[0]
    m = batch * seq
    c_b = ev_conv_w.shape[2]
    n_heads_a = (ev_w_in.shape[2] - 2 * c_b) // (3 * HEAD_DIM)
    n_heads_c = od_b_f.shape[1]
    d_a = n_heads_a * HEAD_DIM
    d_c = n_heads_c * HEAD_DIM
    d_ff = ffn_w_down.shape[1]

    xf = x.reshape(m, d)
    tabs = rope_tables(positions)

    for l in range(depth):
        h = rmsnorm(xf, norm_mix[l], BF16)
        if l % 2 == 0:
            e = l // 2
            u = matmul([h], ev_w_in, e, ev_w_in.shape[2], F32, tm=1024, tn=1024)
            y_a = dilated_attention(u, tabs, batch, seq, n_heads_a)
            y_b = conv_module(u, 3 * d_a, c_b, ev_conv_w[e], ev_conv_b[e], ev_ln_g[e], ev_ln_b[e],
                              batch, seq)
            xf = matmul([y_a, y_b], ev_w_out, e, d, F32, res=xf, tm=1024, tn=1024)
        else:
            o = l // 2
            qkv = matmul([h], od_w_in, o, 3 * d_c, BF16, tm=1024, tn=1024,
                         scale=LOG2E * HEAD_DIM ** -0.5, scale_cols=d_c)
            w_f = jnp.pad(od_w_in[o, :, 3 * d_c:], ((0, 0), (0, LANES - n_heads_c)))
            b_f = jnp.pad(od_b_f[o], (0, LANES - n_heads_c)).reshape(1, LANES)
            f_nat = forget_gates(h, w_f, b_f, batch, seq)
            y = fox_attention(qkv, f_nat, batch, seq, n_heads_c)
            xf = matmul([y], od_w_out, o, d, F32, res=xf, tm=1024, tn=1024)
        h = rmsnorm(xf, norm_ffn[l], BF16)
        act = ffn_up(h, ffn_w_up, ffn_conv_w, l, d_ff, seq)
        xf = matmul([act], ffn_w_down, l, d, F32, res=xf, tm=512, tn=512)
    return rmsnorm(xf, norm_final, F32).reshape(batch, seq, d)
```

```python
import functools

import jax
import jax.numpy as jnp
from jax import lax
from jax.experimental import pallas as pl
from jax.experimental.pallas import tpu as pltpu

F32 = jnp.float32
BF16 = jnp.bfloat16

HEAD_DIM = 128
ROT_DIM = 32
ROPE_THETA = 500000.0
CONV_B_WIDTH = 31
CONV_HALO = 32
FFN_CONV_WIDTH = 3
EPS = 1e-6
NEG = -1e30
LOG2E = 1.4426950408889634
DIL_W = 128
LANES = 128
SUBLANES = 8
VMEM_LIMIT = 56 * 1024 * 1024

_NT = (((1,), (1,)), ((), ()))
_TN = (((0,), (0,)), ((), ()))


def _cparams(sem):
    return pltpu.CompilerParams(dimension_semantics=sem, vmem_limit_bytes=VMEM_LIMIT)


def _rmsnorm_kernel(x_ref, g_ref, o_ref):
    x = x_ref[...]
    ms = jnp.mean(x * x, axis=-1, keepdims=True)
    o_ref[...] = (x * lax.rsqrt(ms + EPS) * g_ref[...]).astype(o_ref.dtype)


def rmsnorm(x, g, out_dtype):
    m, d = x.shape
    tm = 512
    return pl.pallas_call(
        _rmsnorm_kernel,
        grid=(m // tm,),
        in_specs=[pl.BlockSpec((tm, d), lambda i: (i, 0)),
                  pl.BlockSpec((1, d), lambda i: (0, 0))],
        out_specs=pl.BlockSpec((tm, d), lambda i: (i, 0)),
        out_shape=jax.ShapeDtypeStruct((m, d), out_dtype),
        compiler_params=_cparams(("parallel",)),
        name="rmsnorm",
    )(x, g.reshape(1, d))


def _mm_kernel(*refs, n_ops, has_res, scale, n_scale_tiles, w_t):
    a_refs = refs[:n_ops]
    w_refs = refs[n_ops:2 * n_ops]
    res_ref = refs[2 * n_ops] if has_res else None
    o_ref = refs[2 * n_ops + has_res]
    wb_refs = refs[2 * n_ops + has_res + 1:]

    @pl.when(pl.program_id(1) == 0)
    def _():
        for w_ref, wb_ref in zip(w_refs, wb_refs):
            wb_ref[...] = w_ref[...].astype(BF16)

    dims = _NT if w_t else (((1,), (0,)), ((), ()))
    acc = lax.dot_general(a_refs[0][...], wb_refs[0][...], dims, preferred_element_type=F32)
    for a_ref, wb_ref in zip(a_refs[1:], wb_refs[1:]):
        acc = acc + lax.dot_general(a_ref[...], wb_ref[...], dims, preferred_element_type=F32)
    if scale is not None:
        acc = acc * jnp.where(pl.program_id(0) < n_scale_tiles, scale, 1.0).astype(F32)
    if has_res:
        acc = acc + res_ref[...]
    o_ref[...] = acc.astype(o_ref.dtype)


def matmul(a_list, w, layer, n_cols, out_dtype, *, res=None, tm, tn, scale=None, scale_cols=0,
           w_t=False):
    m = a_list[0].shape[0]
    kk = a_list[0].shape[1]
    n_ops = len(a_list)
    assert all(a.shape == (m, kk) for a in a_list) and w.shape[2 if w_t else 1] == n_ops * kk
    assert m % tm == 0 and n_cols % tn == 0 and scale_cols % tn == 0
    in_specs = [pl.BlockSpec((tm, kk), lambda j, i: (i, 0)) for _ in a_list]
    if w_t:
        in_specs += [pl.BlockSpec((None, tn, kk), functools.partial(lambda j, i, r: (layer, j, r), r=r))
                     for r in range(n_ops)]
    else:
        in_specs += [pl.BlockSpec((None, kk, tn), functools.partial(lambda j, i, r: (layer, r, j), r=r))
                     for r in range(n_ops)]
    args = list(a_list) + [w] * n_ops
    if res is not None:
        in_specs.append(pl.BlockSpec((tm, tn), lambda j, i: (i, j)))
        args.append(res)
    kern = functools.partial(_mm_kernel, n_ops=n_ops, has_res=res is not None, scale=scale,
                             n_scale_tiles=scale_cols // tn, w_t=w_t)
    return pl.pallas_call(
        kern,
        grid=(n_cols // tn, m // tm),
        in_specs=in_specs,
        out_specs=pl.BlockSpec((tm, tn), lambda j, i: (i, j)),
        out_shape=jax.ShapeDtypeStruct((m, n_cols), out_dtype),
        scratch_shapes=[pltpu.VMEM((tn, kk) if w_t else (kk, tn), BF16) for _ in range(n_ops)],
        compiler_params=_cparams(("parallel", "arbitrary")),
        name="matmul",
    )(*args)


def _rope_table_kernel(pos_ref, inv_ref, c_ref, sa_ref, sb_ref):
    ang = pos_ref[...].astype(F32) * inv_ref[...]
    lane = lax.broadcasted_iota(jnp.int32, ang.shape, 1)
    cos = jnp.cos(ang)
    sin = jnp.sin(ang)
    half = ROT_DIM // 2
    c_ref[...] = jnp.where(lane < ROT_DIM, cos, 1.0)
    sa_ref[...] = jnp.where(lane < half, -sin, 0.0)
    sb_ref[...] = jnp.where(lane < ROT_DIM, jnp.where(lane >= half, sin, 0.0), 0.0)


def rope_tables(positions):
    m = positions.size
    half = ROT_DIM // 2
    inv = jnp.power(jnp.float32(ROPE_THETA), -jnp.arange(half, dtype=F32) * (2.0 / ROT_DIM))
    inv_lane = jnp.concatenate([inv, inv, jnp.zeros((LANES - ROT_DIM,), F32)]).reshape(1, LANES)
    tm = 1024
    shp = jax.ShapeDtypeStruct((m, LANES), F32)
    spec = pl.BlockSpec((tm, LANES), lambda i: (i, 0))
    return pl.pallas_call(
        _rope_table_kernel,
        grid=(m // tm,),
        in_specs=[pl.BlockSpec((tm, 1), lambda i: (i, 0)),
                  pl.BlockSpec((1, LANES), lambda i: (0, 0))],
        out_specs=[spec, spec, spec],
        out_shape=[shp, shp, shp],
        compiler_params=_cparams(("parallel",)),
        name="rope_tables",
    )(positions.reshape(m, 1), inv_lane)


def _dilated_kernel(q_ref, k_ref, v_ref, c_ref, sa_ref, sb_ref, o_ref,
                    qr, kr, o1, o2, o3, l1, l2, l3, s_scr, p_scr, *, seq):
    half = ROT_DIM // 2
    c = c_ref[...]
    sa = sa_ref[...]
    sb = sb_ref[...]

    def rope(x):
        return x * c + pltpu.roll(x, LANES - half, 1) * sa + pltpu.roll(x, half, 1) * sb

    qr[...] = rope(q_ref[...]) * (HEAD_DIM ** -0.5)
    kr[...] = rope(k_ref[...])

    blocks = []
    for dst, dil in ((0, 1), (1, 4), (2, 16)):
        for r in range(dil):
            for n in range(seq // (dil * DIL_W)):
                q0 = r + n * dil * DIL_W
                blocks.append((dst, q0, q0 - dil * DIL_W, 2 * DIL_W, dil) if n else
                              (dst, q0, q0, DIL_W, dil))
    outs = ((o1, l1), (o2, l2), (o3, l3))

    def sl(start, n, stride):
        return pl.ds(start, n) if stride == 1 else pl.ds(start, n, stride=stride)

    i = lax.broadcasted_iota(jnp.int32, (DIL_W, 2 * DIL_W), 0)
    j = lax.broadcasted_iota(jnp.int32, (DIL_W, 2 * DIL_W), 1)
    dist = DIL_W + i - j
    band2 = (dist >= 0) & (dist <= DIL_W)
    band1 = (lax.broadcasted_iota(jnp.int32, (DIL_W, DIL_W), 0)
             >= lax.broadcasted_iota(jnp.int32, (DIL_W, DIL_W), 1))

    def scores(t):
        _, q0, k0, nk, st = blocks[t]
        qs = qr[sl(q0, DIL_W, st), :].astype(BF16)
        ks = kr[sl(k0, nk, st), :].astype(BF16)
        s_scr[t % 2, :, :nk] = lax.dot_general(qs, ks, _NT, preferred_element_type=F32)

    def softmax(t):
        nk = blocks[t][3]
        s = jnp.where(band1 if nk == DIL_W else band2, s_scr[t % 2, :, :nk], NEG)
        m = jnp.max(s, axis=-1, keepdims=True)
        p = jnp.exp(s - m)
        p_scr[t % 2, :, :nk] = p.astype(BF16)
        return m, jnp.sum(p, axis=-1, keepdims=True)

    def finish(t, m, den):
        dst, q0, k0, nk, st = blocks[t]
        vs = v_ref[sl(k0, nk, st), :].astype(BF16)
        o = jnp.dot(p_scr[t % 2, :, :nk], vs, preferred_element_type=F32) * (1.0 / den)
        o_dst, l_dst = outs[dst]
        o_dst[sl(q0, DIL_W, st), :] = o
        l_dst[sl(q0, DIL_W, st), :] = jnp.broadcast_to(m + jnp.log(den), (DIL_W, LANES))

    nblocks = len(blocks)
    scores(0)
    scores(1)
    stats = softmax(0)
    for t in range(nblocks):
        if t + 2 < nblocks:
            scores(t + 2)
        finish(t, *stats)
        if t + 1 < nblocks:
            stats = softmax(t + 1)

    rows = 256

    def merge_body(t, _):
        rs = pl.ds(pl.multiple_of(t * rows, rows), rows)
        la, lb, lc = l1[rs, :], l2[rs, :], l3[rs, :]
        m = jnp.maximum(jnp.maximum(la, lb), lc)
        ea, eb, ec = jnp.exp(la - m), jnp.exp(lb - m), jnp.exp(lc - m)
        inv = 1.0 / (ea + eb + ec)
        out = (ea * inv) * o1[rs, :] + (eb * inv) * o2[rs, :] + (ec * inv) * o3[rs, :]
        o_ref[rs, :] = out.astype(o_ref.dtype)
        return 0
    lax.fori_loop(0, seq // rows, merge_body, 0)


def dilated_attention(u, tabs, batch, seq, n_heads):
    m = batch * seq
    blk = (seq, HEAD_DIM)
    tab_spec = pl.BlockSpec(blk, lambda b, h: (b, 0))
    scratch = [pltpu.VMEM(blk, F32) for _ in range(8)]
    scratch += [pltpu.VMEM((2, DIL_W, 2 * DIL_W), F32), pltpu.VMEM((2, DIL_W, 2 * DIL_W), BF16)]
    return pl.pallas_call(
        functools.partial(_dilated_kernel, seq=seq),
        grid=(batch, n_heads),
        in_specs=[pl.BlockSpec(blk, lambda b, h: (b, h)),
                  pl.BlockSpec(blk, lambda b, h: (b, n_heads + h)),
                  pl.BlockSpec(blk, lambda b, h: (b, 2 * n_heads + h)),
                  tab_spec, tab_spec, tab_spec],
        out_specs=pl.BlockSpec(blk, lambda b, h: (b, h)),
        out_shape=jax.ShapeDtypeStruct((m, n_heads * HEAD_DIM), BF16),
        scratch_shapes=scratch,
        compiler_params=_cparams(("parallel", "parallel")),
        name="dilated_attention",
    )(u, u, u, *tabs)


def _convmod_kernel(a_ref, g_ref, ah_ref, gh_ref, cw_ref, cb_ref, lng_ref, lnb_ref, o_ref,
                    gs, cs, *, ts, ch):
    i = pl.program_id(1)
    halo = ah_ref[...] * jax.nn.sigmoid(gh_ref[...])
    halo = jnp.where(i > 0, halo, 0.0)
    cur = a_ref[...] * jax.nn.sigmoid(g_ref[...])
    for b in range(SUBLANES):
        gs[b, b:b + CONV_HALO, :] = halo
        gs[b, b + CONV_HALO:b + CONV_HALO + ts, :] = cur

    rb = 64
    first = CONV_HALO - (CONV_B_WIDTH - 1)
    for c in range(ch // LANES):
        cl = slice(c * LANES, (c + 1) * LANES)
        for r in range(ts // rb):
            acc = jnp.zeros((rb, LANES), F32)
            for k in range(CONV_B_WIDTH):
                b = -(first + k) % SUBLANES
                start = r * rb + first + k + b
                acc = acc + cw_ref[k:k + 1, cl] * gs[b, start:start + rb, cl]
            cs[r * rb:(r + 1) * rb, cl] = acc + cb_ref[:, cl]

    lrows = 32

    def ln_body(t, _):
        rs = pl.ds(pl.multiple_of(t * lrows, lrows), lrows)
        x = cs[rs, :]
        mu = jnp.mean(x, axis=-1, keepdims=True)
        xc = x - mu
        var = jnp.mean(xc * xc, axis=-1, keepdims=True)
        y = xc * lax.rsqrt(var + EPS) * lng_ref[...] + lnb_ref[...]
        o_ref[rs, :] = (y * jax.nn.sigmoid(y)).astype(o_ref.dtype)
        return 0
    lax.fori_loop(0, ts // lrows, ln_body, 0, unroll=4)


def conv_module(u, col0, ch, conv_w, conv_b, ln_g, ln_b, batch, seq):
    m = batch * seq
    ts = 256
    nt = seq // ts
    hb = ts // CONV_HALO
    ca = col0 // ch
    cur = lambda cb: pl.BlockSpec((ts, ch), lambda b, i: (b * nt + i, cb))
    halo = lambda cb: pl.BlockSpec((CONV_HALO, ch),
                                   lambda b, i: (jnp.maximum((b * nt + i) * hb - 1, 0), cb))
    vec = lambda rows: pl.BlockSpec((rows, ch), lambda b, i: (0, 0))
    return pl.pallas_call(
        functools.partial(_convmod_kernel, ts=ts, ch=ch),
        grid=(batch, nt),
        in_specs=[cur(ca), cur(ca + 1), halo(ca), halo(ca + 1),
                  vec(CONV_B_WIDTH), vec(1), vec(1), vec(1)],
        out_specs=pl.BlockSpec((ts, ch), lambda b, i: (b * nt + i, 0)),
        out_shape=jax.ShapeDtypeStruct((m, ch), BF16),
        scratch_shapes=[pltpu.VMEM((SUBLANES, CONV_HALO + ts + SUBLANES, ch), F32),
                        pltpu.VMEM((ts, ch), F32)],
        compiler_params=_cparams(("parallel", "parallel")),
        name="conv_module",
    )(u, u, u, u, conv_w, conv_b.reshape(1, ch), ln_g.reshape(1, ch), ln_b.reshape(1, ch))


def _gates_kernel(h_ref, w_ref, b_ref, f_ref, *, seq):
    w_t = w_ref[...].astype(BF16)
    w_t = jnp.concatenate([w_t, jnp.zeros((LANES - w_t.shape[0], w_t.shape[1]), BF16)], axis=0)
    fl = lax.dot_general(h_ref[...], w_t, _NT, preferred_element_type=F32) + b_ref[...]
    logf = jnp.minimum(fl, 0.0) - jnp.log1p(jnp.exp(-jnp.abs(fl)))
    blk = 128
    ii = lax.broadcasted_iota(jnp.int32, (blk, blk), 0)
    jj = lax.broadcasted_iota(jnp.int32, (blk, blk), 1)
    tri = jnp.where(jj <= ii, 1.0, 0.0).astype(BF16)
    carry = jnp.zeros((1, LANES), F32)
    for t in range(seq // blk):
        x = logf[t * blk:(t + 1) * blk, :]
        hi = x.astype(BF16)
        r1 = x - hi.astype(F32)
        mid = r1.astype(BF16)
        lo = (r1 - mid.astype(F32)).astype(BF16)
        cum = (jnp.dot(tri, hi, preferred_element_type=F32)
               + jnp.dot(tri, mid, preferred_element_type=F32)
               + jnp.dot(tri, lo, preferred_element_type=F32)) + carry
        f_ref[t * blk:(t + 1) * blk, :] = cum
        carry = cum[blk - 1:blk, :]


def forget_gates(h, w_in_t, layer, row0, n_heads, b_f, batch, seq):
    m, d = h.shape
    assert row0 % n_heads == 0
    return pl.pallas_call(
        functools.partial(_gates_kernel, seq=seq),
        grid=(batch,),
        in_specs=[pl.BlockSpec((seq, d), lambda b: (b, 0)),
                  pl.BlockSpec((None, n_heads, d), lambda b: (layer, row0 // n_heads, 0)),
                  pl.BlockSpec((1, LANES), lambda b: (0, 0))],
        out_specs=pl.BlockSpec((seq, LANES), lambda b: (b, 0)),
        out_shape=jax.ShapeDtypeStruct((m, LANES), F32),
        compiler_params=_cparams(("parallel",)),
        name="forget_gates",
    )(h, w_in_t, b_f)


def _fox_kernel(q_ref, k_ref, v_ref, fn_ref, o_ref, qa, ka, vt, s_scr, p_scr, *, seq, tq):
    h = pl.program_id(1)
    nblk = seq // tq
    lane = lax.broadcasted_iota(jnp.int32, (seq, LANES), 1)
    f2 = jnp.sum(jnp.where(lane == h, fn_ref[...], 0.0), axis=1, keepdims=True) * LOG2E
    base = [f2[b * tq:b * tq + 1, :] for b in range(nblk)]
    d = jnp.concatenate([f2[b * tq:(b + 1) * tq, :] - base[b] for b in range(nblk)], axis=0)
    hi = d.astype(BF16).astype(F32)
    r1 = d - hi
    mid = r1.astype(BF16).astype(F32)
    lo = (r1 - mid).astype(BF16).astype(F32)
    odd = (lane & 1) == 1
    term = jnp.where(lane < 2, hi, jnp.where(lane < 4, mid, lo))
    live = lane < 6
    aug_q = jnp.where(live, jnp.where(odd, 1.0, term), 0.0)
    aug_k = jnp.where(live, jnp.where(odd, -term, 1.0), 0.0)
    qa[:, :HEAD_DIM] = q_ref[...]
    qa[:, HEAD_DIM:] = aug_q.astype(BF16)
    ka[:, :HEAD_DIM] = k_ref[...]
    ka[:, HEAD_DIM:] = aug_k.astype(BF16)
    vt[...] = v_ref[...].T

    row = lax.broadcasted_iota(jnp.int32, (tq, tq), 0)
    col = lax.broadcasted_iota(jnp.int32, (tq, tq), 1)
    pairs = [(qi, kj) for qi in range(nblk) for kj in range(qi + 1)]

    def scores(qi, kj):
        return lax.dot_general(ka[kj * tq:(kj + 1) * tq, :], qa[qi * tq:(qi + 1) * tq, :], _NT,
                               preferred_element_type=F32)

    def softmax(t, m, l):
        qi, kj = pairs[t]
        s = s_scr[t % 2]
        if kj == qi:
            s = jnp.where(row <= col, s, NEG)
        blk_max = jnp.max(s, axis=0, keepdims=True)
        if kj != qi:
            blk_max = blk_max + (base[qi] - base[kj])
        if kj == 0:
            m_new = blk_max
            alpha = None
        else:
            m_new = jnp.maximum(m, blk_max)
            alpha = jnp.exp2(m - m_new)
        p = jnp.exp2(s - (m_new if kj == qi else m_new - (base[qi] - base[kj])))
        p_scr[t % 2] = p.astype(BF16)
        psum = jnp.sum(p, axis=0, keepdims=True)
        return m_new, (psum if kj == 0 else alpha * l + psum), alpha

    n = len(pairs)
    s_scr[0] = scores(*pairs[0])
    if n > 1:
        s_scr[1] = scores(*pairs[1])
    m, l, alpha = softmax(0, None, None)
    acc = None
    for t, (qi, kj) in enumerate(pairs):
        if t + 2 < n:
            s_scr[t % 2] = scores(*pairs[t + 2])
        pv = jnp.dot(vt[:, kj * tq:(kj + 1) * tq], p_scr[t % 2],
                     preferred_element_type=F32)
        acc = pv if kj == 0 else alpha * acc + pv
        l_t = l
        if t + 1 < n:
            m, l, alpha = softmax(t + 1, m, l)
        if kj == qi:
            o_ref[qi * tq:(qi + 1) * tq, :] = (acc * (1.0 / l_t)).T.astype(o_ref.dtype)


def fox_attention(qkv, f_nat, batch, seq, n_heads):
    m = batch * seq
    tq = 256
    blk = lambda g: pl.BlockSpec((seq, HEAD_DIM), lambda b, h: (b, g * n_heads + h))
    return pl.pallas_call(
        functools.partial(_fox_kernel, seq=seq, tq=tq),
        grid=(batch, n_heads),
        in_specs=[blk(0), blk(1), blk(2), pl.BlockSpec((seq, LANES), lambda b, h: (b, 0))],
        out_specs=pl.BlockSpec((seq, HEAD_DIM), lambda b, h: (b, h)),
        out_shape=jax.ShapeDtypeStruct((m, n_heads * HEAD_DIM), BF16),
        scratch_shapes=[pltpu.VMEM((seq, 2 * HEAD_DIM), BF16), pltpu.VMEM((seq, 2 * HEAD_DIM), BF16),
                        pltpu.VMEM((HEAD_DIM, seq), BF16),
                        pltpu.VMEM((2, tq, tq), F32), pltpu.VMEM((2, tq, tq), BF16)],
        compiler_params=_cparams(("parallel", "parallel")),
        name="fox_attention",
    )(qkv, qkv, qkv, f_nat)


def _ffn_up_kernel(h_ref, wg_ref, wu_ref, cw_ref, o_ref, gs, wgb, wub, *, tm, tiles_per_seq):
    i = pl.program_id(1)

    @pl.when(i == 0)
    def _():
        wgb[...] = wg_ref[...].astype(BF16)
        wub[...] = wu_ref[...].astype(BF16)

    @pl.when(i % tiles_per_seq == 0)
    def _():
        gs[0:SUBLANES, :] = jnp.zeros((SUBLANES, gs.shape[1]), F32)

    h = h_ref[...]
    gs[SUBLANES:, :] = jnp.dot(h, wgb[...], preferred_element_type=F32)
    u = jnp.dot(h, wub[...], preferred_element_type=F32)
    conv = (cw_ref[0:1, :] * gs[SUBLANES - 2:SUBLANES - 2 + tm, :]
            + cw_ref[1:2, :] * gs[SUBLANES - 1:SUBLANES - 1 + tm, :]
            + cw_ref[2:3, :] * gs[SUBLANES:, :])
    o_ref[...] = (conv * jax.nn.sigmoid(conv) * u).astype(o_ref.dtype)
    gs[0:SUBLANES, :] = gs[tm:tm + SUBLANES, :]


def ffn_up(h, w_up, conv_w, layer, d_ff, seq):
    m, d = h.shape
    tm, tn = 1024, 512
    nj = d_ff // tn
    return pl.pallas_call(
        functools.partial(_ffn_up_kernel, tm=tm, tiles_per_seq=seq // tm),
        grid=(nj, m // tm),
        in_specs=[pl.BlockSpec((tm, d), lambda j, i: (i, 0)),
                  pl.BlockSpec((None, d, tn), lambda j, i: (layer, 0, j)),
                  pl.BlockSpec((None, d, tn), lambda j, i: (layer, 0, nj + j)),
                  pl.BlockSpec((None, FFN_CONV_WIDTH, tn), lambda j, i: (layer, 0, j))],
        out_specs=pl.BlockSpec((tm, tn), lambda j, i: (i, j)),
        out_shape=jax.ShapeDtypeStruct((m, d_ff), BF16),
        scratch_shapes=[pltpu.VMEM((tm + SUBLANES, tn), F32),
                        pltpu.VMEM((d, tn), BF16), pltpu.VMEM((d, tn), BF16)],
        compiler_params=_cparams(("parallel", "arbitrary")),
        name="ffn_up",
    )(h, w_up, w_up, conv_w)


def kernel(x, positions, norm_mix, norm_ffn, norm_final, ev_w_in, ev_conv_w, ev_conv_b, ev_ln_g,
           ev_ln_b, ev_w_out, od_w_in, od_b_f, od_w_out, ffn_w_up, ffn_conv_w, ffn_w_down):
    batch, seq, d = x.shape
    depth = norm_mix.shape[0]
    m = batch * seq
    c_b = ev_conv_w.shape[2]
    n_heads_a = (ev_w_in.shape[2] - 2 * c_b) // (3 * HEAD_DIM)
    n_heads_c = od_b_f.shape[1]
    d_a = n_heads_a * HEAD_DIM
    d_c = n_heads_c * HEAD_DIM
    d_ff = ffn_w_down.shape[1]

    xf = x.reshape(m, d)
    tabs = rope_tables(positions)
    od_w_in_t = jnp.swapaxes(od_w_in, 1, 2)

    for l in range(depth):
        h = rmsnorm(xf, norm_mix[l], BF16)
        if l % 2 == 0:
            e = l // 2
            u = matmul([h], ev_w_in, e, ev_w_in.shape[2], F32, tm=1024, tn=1024)
            y_a = dilated_attention(u, tabs, batch, seq, n_heads_a)
            y_b = conv_module(u, 3 * d_a, c_b, ev_conv_w[e], ev_conv_b[e], ev_ln_g[e], ev_ln_b[e],
                              batch, seq)
            xf = matmul([y_a, y_b], ev_w_out, e, d, F32, res=xf, tm=1024, tn=1024)
        else:
            o = l // 2
            qkv = matmul([h], od_w_in_t, o, 3 * d_c, BF16, tm=1024, tn=1024,
                         scale=LOG2E * HEAD_DIM ** -0.5, scale_cols=d_c, w_t=True)
            b_f = jnp.pad(od_b_f[o], (0, LANES - n_heads_c)).reshape(1, LANES)
            f_nat = forget_gates(h, od_w_in_t, o, 3 * d_c, n_heads_c, b_f, batch, seq)
            y = fox_attention(qkv, f_nat, batch, seq, n_heads_c)
            xf = matmul([y], od_w_out, o, d, F32, res=xf, tm=1024, tn=1024)
        h = rmsnorm(xf, norm_ffn[l], BF16)
        act = ffn_up(h, ffn_w_up, ffn_conv_w, l, d_ff, seq)
        xf = matmul([act], ffn_w_down, l, d, F32, res=xf, tm=512, tn=512)
    return rmsnorm(xf, norm_final, F32).reshape(batch, seq, d)
```

```python
import functools

import jax
import jax.numpy as jnp
from jax import lax
from jax.experimental import pallas as pl
from jax.experimental.pallas import tpu as pltpu

F32 = jnp.float32
BF16 = jnp.bfloat16

HEAD_DIM = 128
ROT_DIM = 32
ROPE_THETA = 500000.0
CONV_B_WIDTH = 31
CONV_HALO = 32
FFN_CONV_WIDTH = 3
EPS = 1e-6
NEG = -1e30
LOG2E = 1.4426950408889634
DIL_W = 128
LANES = 128
SUBLANES = 8
VMEM_LIMIT = 56 * 1024 * 1024

_NT = (((1,), (1,)), ((), ()))


def _cparams(sem):
    return pltpu.CompilerParams(dimension_semantics=sem, vmem_limit_bytes=VMEM_LIMIT)


def _rmsnorm_kernel(x_ref, g_ref, o_ref):
    x = x_ref[...]
    ms = jnp.mean(x * x, axis=-1, keepdims=True)
    o_ref[...] = (x * lax.rsqrt(ms + EPS) * g_ref[...]).astype(o_ref.dtype)


def rmsnorm(x, g, out_dtype):
    m, d = x.shape
    tm = 512
    return pl.pallas_call(
        _rmsnorm_kernel,
        grid=(m // tm,),
        in_specs=[pl.BlockSpec((tm, d), lambda i: (i, 0)),
                  pl.BlockSpec((1, d), lambda i: (0, 0))],
        out_specs=pl.BlockSpec((tm, d), lambda i: (i, 0)),
        out_shape=jax.ShapeDtypeStruct((m, d), out_dtype),
        compiler_params=_cparams(("parallel",)),
        name="rmsnorm",
    )(x, g.reshape(1, d))


def _mm_kernel(*refs, n_ops, has_res):
    a_refs = refs[:n_ops]
    w_refs = refs[n_ops:2 * n_ops]
    res_ref = refs[2 * n_ops] if has_res else None
    o_ref = refs[2 * n_ops + has_res]
    wb_refs = refs[2 * n_ops + has_res + 1:]

    @pl.when(pl.program_id(1) == 0)
    def _():
        for w_ref, wb_ref in zip(w_refs, wb_refs):
            wb_ref[...] = w_ref[...].astype(BF16)

    acc = jnp.dot(a_refs[0][...], wb_refs[0][...], preferred_element_type=F32)
    for a_ref, wb_ref in zip(a_refs[1:], wb_refs[1:]):
        acc = acc + jnp.dot(a_ref[...], wb_ref[...], preferred_element_type=F32)
    if has_res:
        acc = acc + res_ref[...]
    o_ref[...] = acc.astype(o_ref.dtype)


def matmul(a_list, w, layer, n_cols, out_dtype, *, res=None, tm, tn, col0=0):
    m = a_list[0].shape[0]
    kk = a_list[0].shape[1]
    n_ops = len(a_list)
    assert all(a.shape == (m, kk) for a in a_list) and w.shape[1] == n_ops * kk
    assert m % tm == 0 and n_cols % tn == 0 and col0 % tn == 0
    jb = col0 // tn
    in_specs = [pl.BlockSpec((tm, kk), lambda j, i: (i, 0)) for _ in a_list]
    in_specs += [pl.BlockSpec((None, kk, tn), functools.partial(lambda j, i, r: (layer, r, jb + j), r=r))
                 for r in range(n_ops)]
    args = list(a_list) + [w] * n_ops
    if res is not None:
        in_specs.append(pl.BlockSpec((tm, tn), lambda j, i: (i, j)))
        args.append(res)
    kern = functools.partial(_mm_kernel, n_ops=n_ops, has_res=res is not None)
    return pl.pallas_call(
        kern,
        grid=(n_cols // tn, m // tm),
        in_specs=in_specs,
        out_specs=pl.BlockSpec((tm, tn), lambda j, i: (i, j)),
        out_shape=jax.ShapeDtypeStruct((m, n_cols), out_dtype),
        scratch_shapes=[pltpu.VMEM((kk, tn), BF16) for _ in range(n_ops)],
        compiler_params=_cparams(("parallel", "arbitrary")),
        name="matmul",
    )(*args)


def _rope_table_kernel(pos_ref, inv_ref, c_ref, sa_ref, sb_ref):
    ang = pos_ref[...].astype(F32) * inv_ref[...]
    lane = lax.broadcasted_iota(jnp.int32, ang.shape, 1)
    cos = jnp.cos(ang)
    sin = jnp.sin(ang)
    half = ROT_DIM // 2
    c_ref[...] = jnp.where(lane < ROT_DIM, cos, 1.0)
    sa_ref[...] = jnp.where(lane < half, -sin, 0.0)
    sb_ref[...] = jnp.where(lane < ROT_DIM, jnp.where(lane >= half, sin, 0.0), 0.0)


def rope_tables(positions):
    m = positions.size
    half = ROT_DIM // 2
    inv = jnp.power(jnp.float32(ROPE_THETA), -jnp.arange(half, dtype=F32) * (2.0 / ROT_DIM))
    inv_lane = jnp.concatenate([inv, inv, jnp.zeros((LANES - ROT_DIM,), F32)]).reshape(1, LANES)
    tm = 1024
    shp = jax.ShapeDtypeStruct((m, LANES), F32)
    spec = pl.BlockSpec((tm, LANES), lambda i: (i, 0))
    return pl.pallas_call(
        _rope_table_kernel,
        grid=(m // tm,),
        in_specs=[pl.BlockSpec((tm, 1), lambda i: (i, 0)),
                  pl.BlockSpec((1, LANES), lambda i: (0, 0))],
        out_specs=[spec, spec, spec],
        out_shape=[shp, shp, shp],
        compiler_params=_cparams(("parallel",)),
        name="rope_tables",
    )(positions.reshape(m, 1), inv_lane)


def _rope(x, c, sa, sb):
    half = ROT_DIM // 2
    return x * c + pltpu.roll(x, LANES - half, 1) * sa + pltpu.roll(x, half, 1) * sb


def _dilated_core(qr, kr, v_ref, o_ref, o1, o2, o3, l1, l2, l3, s_scr, p_scr, *, seq):
    blocks = []
    for dst, dil in ((0, 1), (1, 4), (2, 16)):
        for r in range(dil):
            for n in range(seq // (dil * DIL_W)):
                q0 = r + n * dil * DIL_W
                blocks.append((dst, q0, q0 - dil * DIL_W, 2 * DIL_W, dil) if n else
                              (dst, q0, q0, DIL_W, dil))
    outs = ((o1, l1), (o2, l2), (o3, l3))

    def sl(start, n, stride):
        return pl.ds(start, n) if stride == 1 else pl.ds(start, n, stride=stride)

    i = lax.broadcasted_iota(jnp.int32, (DIL_W, 2 * DIL_W), 0)
    j = lax.broadcasted_iota(jnp.int32, (DIL_W, 2 * DIL_W), 1)
    dist = DIL_W + i - j
    band2 = (dist >= 0) & (dist <= DIL_W)
    band1 = (lax.broadcasted_iota(jnp.int32, (DIL_W, DIL_W), 0)
             >= lax.broadcasted_iota(jnp.int32, (DIL_W, DIL_W), 1))

    def scores(t):
        _, q0, k0, nk, st = blocks[t]
        qs = qr[sl(q0, DIL_W, st), :].astype(BF16)
        ks = kr[sl(k0, nk, st), :].astype(BF16)
        s_scr[t % 2, :, :nk] = lax.dot_general(qs, ks, _NT, preferred_element_type=F32)

    def softmax(t):
        nk = blocks[t][3]
        s = jnp.where(band1 if nk == DIL_W else band2, s_scr[t % 2, :, :nk], NEG)
        m = jnp.max(s, axis=-1, keepdims=True)
        p = jnp.exp(s - m)
        p_scr[t % 2, :, :nk] = p.astype(BF16)
        return m, jnp.sum(p, axis=-1, keepdims=True)

    def finish(t, m, den):
        dst, q0, k0, nk, st = blocks[t]
        vs = v_ref[sl(k0, nk, st), :].astype(BF16)
        o = jnp.dot(p_scr[t % 2, :, :nk], vs, preferred_element_type=F32) * (1.0 / den)
        o_dst, l_dst = outs[dst]
        o_dst[sl(q0, DIL_W, st), :] = o
        l_dst[sl(q0, DIL_W, st), :] = jnp.broadcast_to(m + jnp.log(den), (DIL_W, LANES))

    nblocks = len(blocks)
    scores(0)
    scores(1)
    stats = softmax(0)
    for t in range(nblocks):
        if t + 2 < nblocks:
            scores(t + 2)
        finish(t, *stats)
        if t + 1 < nblocks:
            stats = softmax(t + 1)

    rows = 256

    def merge_body(t, _):
        rs = pl.ds(pl.multiple_of(t * rows, rows), rows)
        la, lb, lc = l1[rs, :], l2[rs, :], l3[rs, :]
        m = jnp.maximum(jnp.maximum(la, lb), lc)
        ea, eb, ec = jnp.exp(la - m), jnp.exp(lb - m), jnp.exp(lc - m)
        inv = 1.0 / (ea + eb + ec)
        out = (ea * inv) * o1[rs, :] + (eb * inv) * o2[rs, :] + (ec * inv) * o3[rs, :]
        o_ref[rs, :] = out.astype(o_ref.dtype)
        return 0
    lax.fori_loop(0, seq // rows, merge_body, 0)


def _even_attn_kernel(hn_ref, wq_ref, wk_ref, wv_ref, c_ref, sa_ref, sb_ref, o_ref, wb, r_scr,
                      qr, kr, vr, o1, o2, o3, l1, l2, l3, s_scr, p_scr, *, seq):
    @pl.when(pl.program_id(0) == 0)
    def _():
        r_scr[...] = jnp.zeros(r_scr.shape, F32)

    c, sa, sb = c_ref[...], sa_ref[...], sb_ref[...]
    qr[...] = _rope(r_scr[:, :HEAD_DIM], c, sa, sb) * (HEAD_DIM ** -0.5)
    kr[...] = _rope(r_scr[:, HEAD_DIM:2 * HEAD_DIM], c, sa, sb)
    vr[...] = r_scr[:, 2 * HEAD_DIM:]

    wb[:, :HEAD_DIM] = wq_ref[...].astype(BF16)
    wb[:, HEAD_DIM:2 * HEAD_DIM] = wk_ref[...].astype(BF16)
    wb[:, 2 * HEAD_DIM:] = wv_ref[...].astype(BF16)
    r_scr[...] = jnp.dot(hn_ref[...], wb[...], preferred_element_type=F32)

    _dilated_core(qr, kr, vr, o_ref, o1, o2, o3, l1, l2, l3, s_scr, p_scr, seq=seq)


def even_attention(hn, w_in, layer, tabs, batch, seq, n_heads):
    m, d = hn.shape
    n = batch * n_heads
    proj = lambda t: jnp.minimum(t, n - 1)
    attn = lambda t: jnp.maximum(t - 1, 0)
    blk = (seq, HEAD_DIM)
    wspec = lambda g: pl.BlockSpec((None, d, HEAD_DIM), lambda t: (layer, 0, g * n_heads + proj(t) % n_heads))
    tab_spec = pl.BlockSpec(blk, lambda t: (attn(t) // n_heads, 0))
    scratch = [pltpu.VMEM((d, 3 * HEAD_DIM), BF16), pltpu.VMEM((seq, 3 * HEAD_DIM), F32)]
    scratch += [pltpu.VMEM(blk, F32) for _ in range(9)]
    scratch += [pltpu.VMEM((2, DIL_W, 2 * DIL_W), F32), pltpu.VMEM((2, DIL_W, 2 * DIL_W), BF16)]
    return pl.pallas_call(
        functools.partial(_even_attn_kernel, seq=seq),
        grid=(n + 1,),
        in_specs=[pl.BlockSpec((seq, d), lambda t: (proj(t) // n_heads, 0)),
                  wspec(0), wspec(1), wspec(2), tab_spec, tab_spec, tab_spec],
        out_specs=pl.BlockSpec(blk, lambda t: (attn(t) // n_heads, attn(t) % n_heads)),
        out_shape=jax.ShapeDtypeStruct((m, n_heads * HEAD_DIM), BF16),
        scratch_shapes=scratch,
        compiler_params=_cparams(("arbitrary",)),
        name="even_attention",
    )(hn, w_in, w_in, w_in, *tabs)


def _convmod_kernel(a_ref, g_ref, ah_ref, gh_ref, cw_ref, cb_ref, lng_ref, lnb_ref, o_ref,
                    gs, cs, *, ts, ch):
    i = pl.program_id(1)
    halo = ah_ref[...] * jax.nn.sigmoid(gh_ref[...])
    halo = jnp.where(i > 0, halo, 0.0)
    cur = a_ref[...] * jax.nn.sigmoid(g_ref[...])
    for b in range(SUBLANES):
        gs[b, b:b + CONV_HALO, :] = halo
        gs[b, b + CONV_HALO:b + CONV_HALO + ts, :] = cur

    rb = 64
    first = CONV_HALO - (CONV_B_WIDTH - 1)
    for c in range(ch // LANES):
        cl = slice(c * LANES, (c + 1) * LANES)
        for r in range(ts // rb):
            acc = jnp.zeros((rb, LANES), F32)
            for k in range(CONV_B_WIDTH):
                b = -(first + k) % SUBLANES
                start = r * rb + first + k + b
                acc = acc + cw_ref[k:k + 1, cl] * gs[b, start:start + rb, cl]
            cs[r * rb:(r + 1) * rb, cl] = acc + cb_ref[:, cl]

    lrows = 32

    def ln_body(t, _):
        rs = pl.ds(pl.multiple_of(t * lrows, lrows), lrows)
        x = cs[rs, :]
        mu = jnp.mean(x, axis=-1, keepdims=True)
        xc = x - mu
        var = jnp.mean(xc * xc, axis=-1, keepdims=True)
        y = xc * lax.rsqrt(var + EPS) * lng_ref[...] + lnb_ref[...]
        o_ref[rs, :] = (y * jax.nn.sigmoid(y)).astype(o_ref.dtype)
        return 0
    lax.fori_loop(0, ts // lrows, ln_body, 0, unroll=4)


def conv_module(u, col0, ch, conv_w, conv_b, ln_g, ln_b, batch, seq):
    m = batch * seq
    ts = 256
    nt = seq // ts
    hb = ts // CONV_HALO
    ca = col0 // ch
    cur = lambda cb: pl.BlockSpec((ts, ch), lambda b, i: (b * nt + i, cb))
    halo = lambda cb: pl.BlockSpec((CONV_HALO, ch),
                                   lambda b, i: (jnp.maximum((b * nt + i) * hb - 1, 0), cb))
    vec = lambda rows: pl.BlockSpec((rows, ch), lambda b, i: (0, 0))
    return pl.pallas_call(
        functools.partial(_convmod_kernel, ts=ts, ch=ch),
        grid=(batch, nt),
        in_specs=[cur(ca), cur(ca + 1), halo(ca), halo(ca + 1),
                  vec(CONV_B_WIDTH), vec(1), vec(1), vec(1)],
        out_specs=pl.BlockSpec((ts, ch), lambda b, i: (b * nt + i, 0)),
        out_shape=jax.ShapeDtypeStruct((m, ch), BF16),
        scratch_shapes=[pltpu.VMEM((SUBLANES, CONV_HALO + ts + SUBLANES, ch), F32),
                        pltpu.VMEM((ts, ch), F32)],
        compiler_params=_cparams(("parallel", "parallel")),
        name="conv_module",
    )(u, u, u, u, conv_w, conv_b.reshape(1, ch), ln_g.reshape(1, ch), ln_b.reshape(1, ch))


def _gates_kernel(h_ref, w_ref, b_ref, f_ref, *, seq):
    w_t = w_ref[...].astype(BF16)
    w_t = jnp.concatenate([w_t, jnp.zeros((LANES - w_t.shape[0], w_t.shape[1]), BF16)], axis=0)
    fl = lax.dot_general(h_ref[...], w_t, _NT, preferred_element_type=F32) + b_ref[...]
    logf = jnp.minimum(fl, 0.0) - jnp.log1p(jnp.exp(-jnp.abs(fl)))
    blk = 128
    ii = lax.broadcasted_iota(jnp.int32, (blk, blk), 0)
    jj = lax.broadcasted_iota(jnp.int32, (blk, blk), 1)
    tri = jnp.where(jj <= ii, 1.0, 0.0).astype(BF16)
    carry = jnp.zeros((1, LANES), F32)
    for t in range(seq // blk):
        x = logf[t * blk:(t + 1) * blk, :]
        hi = x.astype(BF16)
        r1 = x - hi.astype(F32)
        mid = r1.astype(BF16)
        lo = (r1 - mid.astype(F32)).astype(BF16)
        cum = (jnp.dot(tri, hi, preferred_element_type=F32)
               + jnp.dot(tri, mid, preferred_element_type=F32)
               + jnp.dot(tri, lo, preferred_element_type=F32)) + carry
        f_ref[t * blk:(t + 1) * blk, :] = cum
        carry = cum[blk - 1:blk, :]


def forget_gates(h, w_in_t, layer, row0, n_heads, b_f, batch, seq):
    m, d = h.shape
    assert row0 % n_heads == 0
    return pl.pallas_call(
        functools.partial(_gates_kernel, seq=seq),
        grid=(batch,),
        in_specs=[pl.BlockSpec((seq, d), lambda b: (b, 0)),
                  pl.BlockSpec((None, n_heads, d), lambda b: (layer, row0 // n_heads, 0)),
                  pl.BlockSpec((1, LANES), lambda b: (0, 0))],
        out_specs=pl.BlockSpec((seq, LANES), lambda b: (b, 0)),
        out_shape=jax.ShapeDtypeStruct((m, LANES), F32),
        compiler_params=_cparams(("parallel",)),
        name="forget_gates",
    )(h, w_in_t, b_f)


def _fox_bias_columns(f_col, qa, ka, *, seq, tq):
    nblk = seq // tq
    lane = lax.broadcasted_iota(jnp.int32, (seq, LANES), 1)
    f2 = f_col * LOG2E
    base = [f2[b * tq:b * tq + 1, :] for b in range(nblk)]
    d = jnp.concatenate([f2[b * tq:(b + 1) * tq, :] - base[b] for b in range(nblk)], axis=0)
    hi = d.astype(BF16).astype(F32)
    r1 = d - hi
    mid = r1.astype(BF16).astype(F32)
    lo = (r1 - mid).astype(BF16).astype(F32)
    odd = (lane & 1) == 1
    term = jnp.where(lane < 2, hi, jnp.where(lane < 4, mid, lo))
    live = lane < 6
    qa[:, HEAD_DIM:] = jnp.where(live, jnp.where(odd, 1.0, term), 0.0).astype(BF16)
    ka[:, HEAD_DIM:] = jnp.where(live, jnp.where(odd, -term, 1.0), 0.0).astype(BF16)
    return base


def _fox_head(qa, ka, vt, base, s_scr, p_scr, store, *, seq, tq):
    nblk = seq // tq
    row = lax.broadcasted_iota(jnp.int32, (tq, tq), 0)
    col = lax.broadcasted_iota(jnp.int32, (tq, tq), 1)
    pairs = [(qi, kj) for qi in range(nblk) for kj in range(qi + 1)]

    def scores(qi, kj):
        return lax.dot_general(ka[kj * tq:(kj + 1) * tq, :], qa[qi * tq:(qi + 1) * tq, :], _NT,
                               preferred_element_type=F32)

    def softmax(t, m, l):
        qi, kj = pairs[t]
        s = s_scr[t % 2]
        if kj == qi:
            s = jnp.where(row <= col, s, NEG)
        blk_max = jnp.max(s, axis=0, keepdims=True)
        if kj != qi:
            blk_max = blk_max + (base[qi] - base[kj])
        if kj == 0:
            m_new = blk_max
            alpha = None
        else:
            m_new = jnp.maximum(m, blk_max)
            alpha = jnp.exp2(m - m_new)
        p = jnp.exp2(s - (m_new if kj == qi else m_new - (base[qi] - base[kj])))
        p_scr[t % 2] = p.astype(BF16)
        psum = jnp.sum(p, axis=0, keepdims=True)
        return m_new, (psum if kj == 0 else alpha * l + psum), alpha

    n = len(pairs)
    s_scr[0] = scores(*pairs[0])
    if n > 1:
        s_scr[1] = scores(*pairs[1])
    m, l, alpha = softmax(0, None, None)
    acc = None
    for t, (qi, kj) in enumerate(pairs):
        if t + 2 < n:
            s_scr[t % 2] = scores(*pairs[t + 2])
        pv = jnp.dot(vt[:, kj * tq:(kj + 1) * tq], p_scr[t % 2],
                     preferred_element_type=F32)
        acc = pv if kj == 0 else alpha * acc + pv
        l_t = l
        if t + 1 < n:
            m, l, alpha = softmax(t + 1, m, l)
        if kj == qi:
            store(qi, (acc * (1.0 / l_t)).T)


def _odd_mixer_kernel(hn_ref, wq_ref, wk_ref, wv_ref, fn_ref, o_ref, wb, r_scr, qa, ka, vt, s_scr, p_scr,
                      *, seq, tq, scale, pairs_per_batch):
    t = pl.program_id(0)
    d2 = 2 * HEAD_DIM

    @pl.when(t == 0)
    def _():
        r_scr[...] = jnp.zeros(r_scr.shape, F32)

    for hh in range(2):
        cs = slice(hh * HEAD_DIM, (hh + 1) * HEAD_DIM)
        qa[hh, :, :HEAD_DIM] = (r_scr[:, cs] * scale).astype(BF16)
        ka[hh, :, :HEAD_DIM] = r_scr[:, d2 + hh * HEAD_DIM:d2 + (hh + 1) * HEAD_DIM].astype(BF16)
        vt[hh] = r_scr[:, 2 * d2 + hh * HEAD_DIM:2 * d2 + (hh + 1) * HEAD_DIM].T.astype(BF16)

    wb[0:d2, :] = wq_ref[...].astype(BF16)
    wb[d2:2 * d2, :] = wk_ref[...].astype(BF16)
    wb[2 * d2:3 * d2, :] = wv_ref[...].astype(BF16)
    r_scr[...] = lax.dot_general(hn_ref[...], wb[...], _NT, preferred_element_type=F32)

    pair = jnp.maximum(t - 1, 0) % pairs_per_batch
    lane = lax.broadcasted_iota(jnp.int32, (seq, LANES), 1)
    for hh in range(2):
        f_col = jnp.sum(jnp.where(lane == 2 * pair + hh, fn_ref[...], 0.0), axis=1, keepdims=True)
        base = _fox_bias_columns(f_col, qa.at[hh], ka.at[hh], seq=seq, tq=tq)

        def store(qi, blk, hh=hh):
            o_ref[qi * tq:(qi + 1) * tq, hh * HEAD_DIM:(hh + 1) * HEAD_DIM] = blk.astype(o_ref.dtype)
        _fox_head(qa.at[hh], ka.at[hh], vt.at[hh], base, s_scr.at[hh], p_scr.at[hh], store, seq=seq, tq=tq)


def odd_mixer(hn, w_in_t, layer, f_nat, batch, seq, n_heads):
    m, d = hn.shape
    tq = 256
    d2 = 2 * HEAD_DIM
    ppb = n_heads // 2
    n_pairs = batch * ppb
    proj = lambda t: jnp.minimum(t, n_pairs - 1)
    attn = lambda t: jnp.maximum(t - 1, 0)
    wspec = lambda g: pl.BlockSpec((None, d2, d), lambda t: (layer, g * ppb + proj(t) % ppb, 0))
    return pl.pallas_call(
        functools.partial(_odd_mixer_kernel, seq=seq, tq=tq, scale=LOG2E * HEAD_DIM ** -0.5,
                          pairs_per_batch=ppb),
        grid=(n_pairs + 1,),
        in_specs=[pl.BlockSpec((seq, d), lambda t: (proj(t) // ppb, 0)),
                  wspec(0), wspec(1), wspec(2),
                  pl.BlockSpec((seq, LANES), lambda t: (attn(t) // ppb, 0))],
        out_specs=pl.BlockSpec((seq, d2), lambda t: (attn(t) // ppb, attn(t) % ppb)),
        out_shape=jax.ShapeDtypeStruct((m, n_heads * HEAD_DIM), BF16),
        scratch_shapes=[pltpu.VMEM((3 * d2, d), BF16), pltpu.VMEM((seq, 3 * d2), F32),
                        pltpu.VMEM((2, seq, d2), BF16), pltpu.VMEM((2, seq, d2), BF16),
                        pltpu.VMEM((2, HEAD_DIM, seq), BF16),
                        pltpu.VMEM((2, 2, tq, tq), F32), pltpu.VMEM((2, 2, tq, tq), BF16)],
        compiler_params=_cparams(("arbitrary",)),
        name="odd_mixer",
    )(hn, w_in_t, w_in_t, w_in_t, f_nat)


def _ffn_up_kernel(h_ref, wg_ref, wu_ref, cw_ref, o_ref, gs, wgb, wub, *, tm, tiles_per_seq):
    i = pl.program_id(1)

    @pl.when(i == 0)
    def _():
        wgb[...] = wg_ref[...].astype(BF16)
        wub[...] = wu_ref[...].astype(BF16)

    @pl.when(i % tiles_per_seq == 0)
    def _():
        gs[0:SUBLANES, :] = jnp.zeros((SUBLANES, gs.shape[1]), F32)

    h = h_ref[...]
    gs[SUBLANES:, :] = jnp.dot(h, wgb[...], preferred_element_type=F32)
    u = jnp.dot(h, wub[...], preferred_element_type=F32)
    conv = (cw_ref[0:1, :] * gs[SUBLANES - 2:SUBLANES - 2 + tm, :]
            + cw_ref[1:2, :] * gs[SUBLANES - 1:SUBLANES - 1 + tm, :]
            + cw_ref[2:3, :] * gs[SUBLANES:, :])
    o_ref[...] = (conv * jax.nn.sigmoid(conv) * u).astype(o_ref.dtype)
    gs[0:SUBLANES, :] = gs[tm:tm + SUBLANES, :]


def ffn_up(h, w_up, conv_w, layer, d_ff, seq):
    m, d = h.shape
    tm, tn = 1024, 512
    nj = d_ff // tn
    return pl.pallas_call(
        functools.partial(_ffn_up_kernel, tm=tm, tiles_per_seq=seq // tm),
        grid=(nj, m // tm),
        in_specs=[pl.BlockSpec((tm, d), lambda j, i: (i, 0)),
                  pl.BlockSpec((None, d, tn), lambda j, i: (layer, 0, j)),
                  pl.BlockSpec((None, d, tn), lambda j, i: (layer, 0, nj + j)),
                  pl.BlockSpec((None, FFN_CONV_WIDTH, tn), lambda j, i: (layer, 0, j))],
        out_specs=pl.BlockSpec((tm, tn), lambda j, i: (i, j)),
        out_shape=jax.ShapeDtypeStruct((m, d_ff), BF16),
        scratch_shapes=[pltpu.VMEM((tm + SUBLANES, tn), F32),
                        pltpu.VMEM((d, tn), BF16), pltpu.VMEM((d, tn), BF16)],
        compiler_params=_cparams(("parallel", "arbitrary")),
        name="ffn_up",
    )(h, w_up, w_up, conv_w)


def kernel(x, positions, norm_mix, norm_ffn, norm_final, ev_w_in, ev_conv_w, ev_conv_b, ev_ln_g,
           ev_ln_b, ev_w_out, od_w_in, od_b_f, od_w_out, ffn_w_up, ffn_conv_w, ffn_w_down):
    batch, seq, d = x.shape
    depth = norm_mix.shape[0]
    m = batch * seq
    c_b = ev_conv_w.shape[2]
    n_heads_a = (ev_w_in.shape[2] - 2 * c_b) // (3 * HEAD_DIM)
    n_heads_c = od_b_f.shape[1]
    d_a = n_heads_a * HEAD_DIM
    d_c = n_heads_c * HEAD_DIM
    d_ff = ffn_w_down.shape[1]

    xf = x.reshape(m, d)
    tabs = rope_tables(positions)
    od_w_in_t = jnp.swapaxes(od_w_in, 1, 2)

    for l in range(depth):
        h = rmsnorm(xf, norm_mix[l], BF16)
        if l % 2 == 0:
            e = l // 2
            y_a = even_attention(h, ev_w_in, e, tabs, batch, seq, n_heads_a)
            u = matmul([h], ev_w_in, e, 2 * c_b, F32, tm=1024, tn=1024, col0=3 * d_a)
            y_b = conv_module(u, 0, c_b, ev_conv_w[e], ev_conv_b[e], ev_ln_g[e], ev_ln_b[e],
                              batch, seq)
            xf = matmul([y_a, y_b], ev_w_out, e, d, F32, res=xf, tm=1024, tn=1024)
        else:
            o = l // 2
            b_f = jnp.pad(od_b_f[o], (0, LANES - n_heads_c)).reshape(1, LANES)
            f_nat = forget_gates(h, od_w_in_t, o, 3 * d_c, n_heads_c, b_f, batch, seq)
            y = odd_mixer(h, od_w_in_t, o, f_nat, batch, seq, n_heads_c)
            xf = matmul([y], od_w_out, o, d, F32, res=xf, tm=1024, tn=1024)
        h = rmsnorm(xf, norm_ffn[l], BF16)
        act = ffn_up(h, ffn_w_up, ffn_conv_w, l, d_ff, seq)
        xf = matmul([act], ffn_w_down, l, d, F32, res=xf, tm=512, tn=512)
    return rmsnorm(xf, norm_final, F32).reshape(batch, seq, d)
```

```python
import functools

import jax
import jax.numpy as jnp
from jax import lax
from jax.experimental import pallas as pl
from jax.experimental.pallas import tpu as pltpu

F32 = jnp.float32
BF16 = jnp.bfloat16

HEAD_DIM = 128
ROT_DIM = 32
ROPE_THETA = 500000.0
CONV_B_WIDTH = 31
CONV_HALO = 32
FFN_CONV_WIDTH = 3
EPS = 1e-6
NEG = -1e30
LOG2E = 1.4426950408889634
DIL_W = 128
LANES = 128
SUBLANES = 8
VMEM_LIMIT = 56 * 1024 * 1024

_NT = (((1,), (1,)), ((), ()))


def _cparams(sem):
    return pltpu.CompilerParams(dimension_semantics=sem, vmem_limit_bytes=VMEM_LIMIT)


def _rmsnorm_kernel(x_ref, g_ref, o_ref):
    x = x_ref[...]
    ms = jnp.mean(x * x, axis=-1, keepdims=True)
    o_ref[...] = (x * lax.rsqrt(ms + EPS) * g_ref[...]).astype(o_ref.dtype)


def rmsnorm(x, g, out_dtype):
    m, d = x.shape
    tm = 512
    return pl.pallas_call(
        _rmsnorm_kernel,
        grid=(m // tm,),
        in_specs=[pl.BlockSpec((tm, d), lambda i: (i, 0)),
                  pl.BlockSpec((1, d), lambda i: (0, 0))],
        out_specs=pl.BlockSpec((tm, d), lambda i: (i, 0)),
        out_shape=jax.ShapeDtypeStruct((m, d), out_dtype),
        compiler_params=_cparams(("parallel",)),
        name="rmsnorm",
    )(x, g.reshape(1, d))


def _mm_kernel(*refs, n_ops, has_res, has_norm):
    a_refs = refs[:n_ops]
    w_refs = refs[n_ops:2 * n_ops]
    res_ref = refs[2 * n_ops] if has_res else None
    g_ref = refs[2 * n_ops + has_res] if has_norm else None
    o_ref = refs[2 * n_ops + has_res + has_norm]
    hn_ref = refs[2 * n_ops + has_res + has_norm + 1] if has_norm else None
    wb_refs = refs[2 * n_ops + has_res + 2 * has_norm + 1:]

    @pl.when(pl.program_id(1) == 0)
    def _():
        for w_ref, wb_ref in zip(w_refs, wb_refs):
            wb_ref[...] = w_ref[...].astype(BF16)

    acc = jnp.dot(a_refs[0][...], wb_refs[0][...], preferred_element_type=F32)
    for a_ref, wb_ref in zip(a_refs[1:], wb_refs[1:]):
        acc = acc + jnp.dot(a_ref[...], wb_ref[...], preferred_element_type=F32)
    if has_res:
        acc = acc + res_ref[...]
    o_ref[...] = acc.astype(o_ref.dtype)
    if has_norm:
        ms = jnp.mean(acc * acc, axis=-1, keepdims=True)
        hn_ref[...] = (acc * lax.rsqrt(ms + EPS) * g_ref[...]).astype(hn_ref.dtype)


def matmul(a_list, w, layer, n_cols, out_dtype, *, res=None, tm, tn, col0=0, norm_g=None):
    m = a_list[0].shape[0]
    kk = a_list[0].shape[1]
    n_ops = len(a_list)
    assert all(a.shape == (m, kk) for a in a_list) and w.shape[1] == n_ops * kk
    assert m % tm == 0 and n_cols % tn == 0 and col0 % tn == 0
    jb = col0 // tn
    in_specs = [pl.BlockSpec((tm, kk), lambda j, i: (i, 0)) for _ in a_list]
    has_norm = norm_g is not None
    assert not has_norm or tn == n_cols
    wmode = dict(pipeline_mode=pl.Buffered(1)) if has_norm else {}
    in_specs += [pl.BlockSpec((None, kk, tn), functools.partial(lambda j, i, r: (layer, r, jb + j), r=r), **wmode)
                 for r in range(n_ops)]
    args = list(a_list) + [w] * n_ops
    if res is not None:
        in_specs.append(pl.BlockSpec((tm, tn), lambda j, i: (i, j)))
        args.append(res)
    out_spec = pl.BlockSpec((tm, tn), lambda j, i: (i, j))
    out_specs, out_shape = out_spec, jax.ShapeDtypeStruct((m, n_cols), out_dtype)
    if has_norm:
        in_specs.append(pl.BlockSpec((1, tn), lambda j, i: (0, 0)))
        args.append(norm_g.reshape(1, n_cols))
        out_specs, out_shape = [out_spec, out_spec], [out_shape, jax.ShapeDtypeStruct((m, n_cols), BF16)]
    kern = functools.partial(_mm_kernel, n_ops=n_ops, has_res=res is not None, has_norm=has_norm)
    return pl.pallas_call(
        kern,
        grid=(n_cols // tn, m // tm),
        in_specs=in_specs,
        out_specs=out_specs,
        out_shape=out_shape,
        scratch_shapes=[pltpu.VMEM((kk, tn), BF16) for _ in range(n_ops)],
        compiler_params=_cparams(("parallel", "arbitrary")),
        name="matmul",
    )(*args)


def _rope_table_kernel(pos_ref, inv_ref, c_ref, sa_ref, sb_ref):
    ang = pos_ref[...].astype(F32) * inv_ref[...]
    lane = lax.broadcasted_iota(jnp.int32, ang.shape, 1)
    cos = jnp.cos(ang)
    sin = jnp.sin(ang)
    half = ROT_DIM // 2
    c_ref[...] = jnp.where(lane < ROT_DIM, cos, 1.0)
    sa_ref[...] = jnp.where(lane < half, -sin, 0.0)
    sb_ref[...] = jnp.where(lane < ROT_DIM, jnp.where(lane >= half, sin, 0.0), 0.0)


def rope_tables(positions):
    m = positions.size
    half = ROT_DIM // 2
    inv = jnp.power(jnp.float32(ROPE_THETA), -jnp.arange(half, dtype=F32) * (2.0 / ROT_DIM))
    inv_lane = jnp.concatenate([inv, inv, jnp.zeros((LANES - ROT_DIM,), F32)]).reshape(1, LANES)
    tm = 1024
    shp = jax.ShapeDtypeStruct((m, LANES), F32)
    spec = pl.BlockSpec((tm, LANES), lambda i: (i, 0))
    return pl.pallas_call(
        _rope_table_kernel,
        grid=(m // tm,),
        in_specs=[pl.BlockSpec((tm, 1), lambda i: (i, 0)),
                  pl.BlockSpec((1, LANES), lambda i: (0, 0))],
        out_specs=[spec, spec, spec],
        out_shape=[shp, shp, shp],
        compiler_params=_cparams(("parallel",)),
        name="rope_tables",
    )(positions.reshape(m, 1), inv_lane)


def _rope(x, c, sa, sb):
    half = ROT_DIM // 2
    return x * c + pltpu.roll(x, LANES - half, 1) * sa + pltpu.roll(x, half, 1) * sb


def _dilated_core(qr, kr, v_ref, o_ref, o1, o2, o3, l1, l2, l3, s_scr, p_scr, *, seq):
    blocks = []
    for dst, dil in ((0, 1), (1, 4), (2, 16)):
        for r in range(dil):
            for n in range(seq // (dil * DIL_W)):
                q0 = r + n * dil * DIL_W
                blocks.append((dst, q0, q0 - dil * DIL_W, 2 * DIL_W, dil) if n else
                              (dst, q0, q0, DIL_W, dil))
    outs = ((o1, l1), (o2, l2), (o3, l3))

    def sl(start, n, stride):
        return pl.ds(start, n) if stride == 1 else pl.ds(start, n, stride=stride)

    i = lax.broadcasted_iota(jnp.int32, (DIL_W, 2 * DIL_W), 0)
    j = lax.broadcasted_iota(jnp.int32, (DIL_W, 2 * DIL_W), 1)
    dist = DIL_W + i - j
    band2 = (dist >= 0) & (dist <= DIL_W)
    band1 = (lax.broadcasted_iota(jnp.int32, (DIL_W, DIL_W), 0)
             >= lax.broadcasted_iota(jnp.int32, (DIL_W, DIL_W), 1))

    def scores(t):
        _, q0, k0, nk, st = blocks[t]
        qs = qr[sl(q0, DIL_W, st), :].astype(BF16)
        ks = kr[sl(k0, nk, st), :].astype(BF16)
        s_scr[t % 2, :, :nk] = lax.dot_general(qs, ks, _NT, preferred_element_type=F32)

    def softmax(t):
        nk = blocks[t][3]
        s = jnp.where(band1 if nk == DIL_W else band2, s_scr[t % 2, :, :nk], NEG)
        m = jnp.max(s, axis=-1, keepdims=True)
        p = jnp.exp(s - m)
        p_scr[t % 2, :, :nk] = p.astype(BF16)
        return m, jnp.sum(p, axis=-1, keepdims=True)

    def finish(t, m, den):
        dst, q0, k0, nk, st = blocks[t]
        vs = v_ref[sl(k0, nk, st), :].astype(BF16)
        o = jnp.dot(p_scr[t % 2, :, :nk], vs, preferred_element_type=F32) * (1.0 / den)
        o_dst, l_dst = outs[dst]
        o_dst[sl(q0, DIL_W, st), :] = o
        l_dst[sl(q0, DIL_W, st), :] = jnp.broadcast_to(m + jnp.log(den), (DIL_W, LANES))

    nblocks = len(blocks)
    scores(0)
    scores(1)
    stats = softmax(0)
    for t in range(nblocks):
        if t + 2 < nblocks:
            scores(t + 2)
        finish(t, *stats)
        if t + 1 < nblocks:
            stats = softmax(t + 1)

    rows = 256

    def merge_body(t, _):
        rs = pl.ds(pl.multiple_of(t * rows, rows), rows)
        la, lb, lc = l1[rs, :], l2[rs, :], l3[rs, :]
        m = jnp.maximum(jnp.maximum(la, lb), lc)
        ea, eb, ec = jnp.exp(la - m), jnp.exp(lb - m), jnp.exp(lc - m)
        inv = 1.0 / (ea + eb + ec)
        out = (ea * inv) * o1[rs, :] + (eb * inv) * o2[rs, :] + (ec * inv) * o3[rs, :]
        o_ref[rs, :] = out.astype(o_ref.dtype)
        return 0
    lax.fori_loop(0, seq // rows, merge_body, 0)


def _even_attn_kernel(hn_ref, wq_ref, wk_ref, wv_ref, c_ref, sa_ref, sb_ref, o_ref, wb, r_scr,
                      qr, kr, vr, o1, o2, o3, l1, l2, l3, s_scr, p_scr, *, seq):
    @pl.when(pl.program_id(0) == 0)
    def _():
        r_scr[...] = jnp.zeros(r_scr.shape, F32)

    c, sa, sb = c_ref[...], sa_ref[...], sb_ref[...]
    qr[...] = _rope(r_scr[:, :HEAD_DIM], c, sa, sb) * (HEAD_DIM ** -0.5)
    kr[...] = _rope(r_scr[:, HEAD_DIM:2 * HEAD_DIM], c, sa, sb)
    vr[...] = r_scr[:, 2 * HEAD_DIM:]

    wb[:, :HEAD_DIM] = wq_ref[...].astype(BF16)
    wb[:, HEAD_DIM:2 * HEAD_DIM] = wk_ref[...].astype(BF16)
    wb[:, 2 * HEAD_DIM:] = wv_ref[...].astype(BF16)
    r_scr[...] = jnp.dot(hn_ref[...], wb[...], preferred_element_type=F32)

    _dilated_core(qr, kr, vr, o_ref, o1, o2, o3, l1, l2, l3, s_scr, p_scr, seq=seq)


def even_attention(hn, w_in, layer, tabs, batch, seq, n_heads):
    m, d = hn.shape
    n = batch * n_heads
    proj = lambda t: jnp.minimum(t, n - 1)
    attn = lambda t: jnp.maximum(t - 1, 0)
    blk = (seq, HEAD_DIM)
    wspec = lambda g: pl.BlockSpec((None, d, HEAD_DIM), lambda t: (layer, 0, g * n_heads + proj(t) % n_heads))
    tab_spec = pl.BlockSpec(blk, lambda t: (attn(t) // n_heads, 0))
    scratch = [pltpu.VMEM((d, 3 * HEAD_DIM), BF16), pltpu.VMEM((seq, 3 * HEAD_DIM), F32)]
    scratch += [pltpu.VMEM(blk, F32) for _ in range(9)]
    scratch += [pltpu.VMEM((2, DIL_W, 2 * DIL_W), F32), pltpu.VMEM((2, DIL_W, 2 * DIL_W), BF16)]
    return pl.pallas_call(
        functools.partial(_even_attn_kernel, seq=seq),
        grid=(n + 1,),
        in_specs=[pl.BlockSpec((seq, d), lambda t: (proj(t) // n_heads, 0)),
                  wspec(0), wspec(1), wspec(2), tab_spec, tab_spec, tab_spec],
        out_specs=pl.BlockSpec(blk, lambda t: (attn(t) // n_heads, attn(t) % n_heads)),
        out_shape=jax.ShapeDtypeStruct((m, n_heads * HEAD_DIM), BF16),
        scratch_shapes=scratch,
        compiler_params=_cparams(("arbitrary",)),
        name="even_attention",
    )(hn, w_in, w_in, w_in, *tabs)


def _convmod_kernel(a_ref, g_ref, ah_ref, gh_ref, cw_ref, cb_ref, lng_ref, lnb_ref, o_ref,
                    gs, cs, *, ts, ch):
    i = pl.program_id(1)
    halo = ah_ref[...] * jax.nn.sigmoid(gh_ref[...])
    halo = jnp.where(i > 0, halo, 0.0)
    cur = a_ref[...] * jax.nn.sigmoid(g_ref[...])
    for b in range(SUBLANES):
        gs[b, b:b + CONV_HALO, :] = halo
        gs[b, b + CONV_HALO:b + CONV_HALO + ts, :] = cur

    rb = 64
    first = CONV_HALO - (CONV_B_WIDTH - 1)
    for c in range(ch // LANES):
        cl = slice(c * LANES, (c + 1) * LANES)
        for r in range(ts // rb):
            acc = jnp.zeros((rb, LANES), F32)
            for k in range(CONV_B_WIDTH):
                b = -(first + k) % SUBLANES
                start = r * rb + first + k + b
                acc = acc + cw_ref[k:k + 1, cl] * gs[b, start:start + rb, cl]
            cs[r * rb:(r + 1) * rb, cl] = acc + cb_ref[:, cl]

    lrows = 32

    def ln_body(t, _):
        rs = pl.ds(pl.multiple_of(t * lrows, lrows), lrows)
        x = cs[rs, :]
        mu = jnp.mean(x, axis=-1, keepdims=True)
        xc = x - mu
        var = jnp.mean(xc * xc, axis=-1, keepdims=True)
        y = xc * lax.rsqrt(var + EPS) * lng_ref[...] + lnb_ref[...]
        o_ref[rs, :] = (y * jax.nn.sigmoid(y)).astype(o_ref.dtype)
        return 0
    lax.fori_loop(0, ts // lrows, ln_body, 0, unroll=4)


def conv_module(u, col0, ch, conv_w, conv_b, ln_g, ln_b, batch, seq):
    m = batch * seq
    ts = 256
    nt = seq // ts
    hb = ts // CONV_HALO
    ca = col0 // ch
    cur = lambda cb: pl.BlockSpec((ts, ch), lambda b, i: (b * nt + i, cb))
    halo = lambda cb: pl.BlockSpec((CONV_HALO, ch),
                                   lambda b, i: (jnp.maximum((b * nt + i) * hb - 1, 0), cb))
    vec = lambda rows: pl.BlockSpec((rows, ch), lambda b, i: (0, 0))
    return pl.pallas_call(
        functools.partial(_convmod_kernel, ts=ts, ch=ch),
        grid=(batch, nt),
        in_specs=[cur(ca), cur(ca + 1), halo(ca), halo(ca + 1),
                  vec(CONV_B_WIDTH), vec(1), vec(1), vec(1)],
        out_specs=pl.BlockSpec((ts, ch), lambda b, i: (b * nt + i, 0)),
        out_shape=jax.ShapeDtypeStruct((m, ch), BF16),
        scratch_shapes=[pltpu.VMEM((SUBLANES, CONV_HALO + ts + SUBLANES, ch), F32),
                        pltpu.VMEM((ts, ch), F32)],
        compiler_params=_cparams(("parallel", "parallel")),
        name="conv_module",
    )(u, u, u, u, conv_w, conv_b.reshape(1, ch), ln_g.reshape(1, ch), ln_b.reshape(1, ch))


def _gates_kernel(h_ref, w_ref, b_ref, f_ref, *, seq):
    w_t = w_ref[...].astype(BF16)
    w_t = jnp.concatenate([w_t, jnp.zeros((LANES - w_t.shape[0], w_t.shape[1]), BF16)], axis=0)
    fl = lax.dot_general(h_ref[...], w_t, _NT, preferred_element_type=F32) + b_ref[...]
    logf = jnp.minimum(fl, 0.0) - jnp.log1p(jnp.exp(-jnp.abs(fl)))
    blk = 128
    ii = lax.broadcasted_iota(jnp.int32, (blk, blk), 0)
    jj = lax.broadcasted_iota(jnp.int32, (blk, blk), 1)
    tri = jnp.where(jj <= ii, 1.0, 0.0).astype(BF16)
    carry = jnp.zeros((1, LANES), F32)
    for t in range(seq // blk):
        x = logf[t * blk:(t + 1) * blk, :]
        hi = x.astype(BF16)
        r1 = x - hi.astype(F32)
        mid = r1.astype(BF16)
        lo = (r1 - mid.astype(F32)).astype(BF16)
        cum = (jnp.dot(tri, hi, preferred_element_type=F32)
               + jnp.dot(tri, mid, preferred_element_type=F32)
               + jnp.dot(tri, lo, preferred_element_type=F32)) + carry
        f_ref[t * blk:(t + 1) * blk, :] = cum
        carry = cum[blk - 1:blk, :]


def forget_gates(h, w_in_t, layer, row0, n_heads, b_f, batch, seq):
    m, d = h.shape
    assert row0 % n_heads == 0
    return pl.pallas_call(
        functools.partial(_gates_kernel, seq=seq),
        grid=(batch,),
        in_specs=[pl.BlockSpec((seq, d), lambda b: (b, 0)),
                  pl.BlockSpec((None, n_heads, d), lambda b: (layer, row0 // n_heads, 0)),
                  pl.BlockSpec((1, LANES), lambda b: (0, 0))],
        out_specs=pl.BlockSpec((seq, LANES), lambda b: (b, 0)),
        out_shape=jax.ShapeDtypeStruct((m, LANES), F32),
        compiler_params=_cparams(("parallel",)),
        name="forget_gates",
    )(h, w_in_t, b_f)


def _fox_bias_columns(f_col, qa, ka, *, seq, tq):
    nblk = seq // tq
    lane = lax.broadcasted_iota(jnp.int32, (seq, LANES), 1)
    f2 = f_col * LOG2E
    base = [f2[b * tq:b * tq + 1, :] for b in range(nblk)]
    d = jnp.concatenate([f2[b * tq:(b + 1) * tq, :] - base[b] for b in range(nblk)], axis=0)
    hi = d.astype(BF16).astype(F32)
    r1 = d - hi
    mid = r1.astype(BF16).astype(F32)
    lo = (r1 - mid).astype(BF16).astype(F32)
    odd = (lane & 1) == 1
    term = jnp.where(lane < 2, hi, jnp.where(lane < 4, mid, lo))
    live = lane < 6
    qa[:, HEAD_DIM:] = jnp.where(live, jnp.where(odd, 1.0, term), 0.0).astype(BF16)
    ka[:, HEAD_DIM:] = jnp.where(live, jnp.where(odd, -term, 1.0), 0.0).astype(BF16)
    return base


def _fox_head(qa, ka, vt, base, s_scr, p_scr, store, *, seq, tq):
    nblk = seq // tq
    row = lax.broadcasted_iota(jnp.int32, (tq, tq), 0)
    col = lax.broadcasted_iota(jnp.int32, (tq, tq), 1)
    pairs = [(qi, kj) for qi in range(nblk) for kj in range(qi + 1)]

    def scores(qi, kj):
        return lax.dot_general(ka[kj * tq:(kj + 1) * tq, :], qa[qi * tq:(qi + 1) * tq, :], _NT,
                               preferred_element_type=F32)

    def softmax(t, m, l):
        qi, kj = pairs[t]
        s = s_scr[t % 2]
        if kj == qi:
            s = jnp.where(row <= col, s, NEG)
        blk_max = jnp.max(s, axis=0, keepdims=True)
        if kj != qi:
            blk_max = blk_max + (base[qi] - base[kj])
        if kj == 0:
            m_new = blk_max
            alpha = None
        else:
            m_new = jnp.maximum(m, blk_max)
            alpha = jnp.exp2(m - m_new)
        p = jnp.exp2(s - (m_new if kj == qi else m_new - (base[qi] - base[kj])))
        p_scr[t % 2] = p.astype(BF16)
        psum = jnp.sum(p, axis=0, keepdims=True)
        return m_new, (psum if kj == 0 else alpha * l + psum), alpha

    n = len(pairs)
    s_scr[0] = scores(*pairs[0])
    if n > 1:
        s_scr[1] = scores(*pairs[1])
    m, l, alpha = softmax(0, None, None)
    acc = None
    for t, (qi, kj) in enumerate(pairs):
        if t + 2 < n:
            s_scr[t % 2] = scores(*pairs[t + 2])
        pv = jnp.dot(vt[:, kj * tq:(kj + 1) * tq], p_scr[t % 2],
                     preferred_element_type=F32)
        acc = pv if kj == 0 else alpha * acc + pv
        l_t = l
        if t + 1 < n:
            m, l, alpha = softmax(t + 1, m, l)
        if kj == qi:
            store(qi, (acc * (1.0 / l_t)).T)


def _odd_mixer_kernel(hn_ref, wq_ref, wk_ref, wv_ref, fn_ref, o_ref, wb, r_scr, qa, ka, vt, s_scr, p_scr,
                      *, seq, tq, scale, pairs_per_batch):
    t = pl.program_id(0)
    d2 = 2 * HEAD_DIM

    @pl.when(t == 0)
    def _():
        r_scr[...] = jnp.zeros(r_scr.shape, F32)

    for hh in range(2):
        cs = slice(hh * HEAD_DIM, (hh + 1) * HEAD_DIM)
        qa[hh, :, :HEAD_DIM] = (r_scr[:, cs] * scale).astype(BF16)
        ka[hh, :, :HEAD_DIM] = r_scr[:, d2 + hh * HEAD_DIM:d2 + (hh + 1) * HEAD_DIM].astype(BF16)
        vt[hh] = r_scr[:, 2 * d2 + hh * HEAD_DIM:2 * d2 + (hh + 1) * HEAD_DIM].T.astype(BF16)

    wb[0:d2, :] = wq_ref[...].astype(BF16)
    wb[d2:2 * d2, :] = wk_ref[...].astype(BF16)
    wb[2 * d2:3 * d2, :] = wv_ref[...].astype(BF16)
    r_scr[...] = lax.dot_general(hn_ref[...], wb[...], _NT, preferred_element_type=F32)

    pair = jnp.maximum(t - 1, 0) % pairs_per_batch
    lane = lax.broadcasted_iota(jnp.int32, (seq, LANES), 1)
    for hh in range(2):
        f_col = jnp.sum(jnp.where(lane == 2 * pair + hh, fn_ref[...], 0.0), axis=1, keepdims=True)
        base = _fox_bias_columns(f_col, qa.at[hh], ka.at[hh], seq=seq, tq=tq)

        def store(qi, blk, hh=hh):
            o_ref[qi * tq:(qi + 1) * tq, hh * HEAD_DIM:(hh + 1) * HEAD_DIM] = blk.astype(o_ref.dtype)
        _fox_head(qa.at[hh], ka.at[hh], vt.at[hh], base, s_scr.at[hh], p_scr.at[hh], store, seq=seq, tq=tq)


def odd_mixer(hn, w_in_t, layer, f_nat, batch, seq, n_heads):
    m, d = hn.shape
    tq = 256
    d2 = 2 * HEAD_DIM
    ppb = n_heads // 2
    n_pairs = batch * ppb
    proj = lambda t: jnp.minimum(t, n_pairs - 1)
    attn = lambda t: jnp.maximum(t - 1, 0)
    wspec = lambda g: pl.BlockSpec((None, d2, d), lambda t: (layer, g * ppb + proj(t) % ppb, 0))
    return pl.pallas_call(
        functools.partial(_odd_mixer_kernel, seq=seq, tq=tq, scale=LOG2E * HEAD_DIM ** -0.5,
                          pairs_per_batch=ppb),
        grid=(n_pairs + 1,),
        in_specs=[pl.BlockSpec((seq, d), lambda t: (proj(t) // ppb, 0)),
                  wspec(0), wspec(1), wspec(2),
                  pl.BlockSpec((seq, LANES), lambda t: (attn(t) // ppb, 0))],
        out_specs=pl.BlockSpec((seq, d2), lambda t: (attn(t) // ppb, attn(t) % ppb)),
        out_shape=jax.ShapeDtypeStruct((m, n_heads * HEAD_DIM), BF16),
        scratch_shapes=[pltpu.VMEM((3 * d2, d), BF16), pltpu.VMEM((seq, 3 * d2), F32),
                        pltpu.VMEM((2, seq, d2), BF16), pltpu.VMEM((2, seq, d2), BF16),
                        pltpu.VMEM((2, HEAD_DIM, seq), BF16),
                        pltpu.VMEM((2, 2, tq, tq), F32), pltpu.VMEM((2, 2, tq, tq), BF16)],
        compiler_params=_cparams(("arbitrary",)),
        name="odd_mixer",
    )(hn, w_in_t, w_in_t, w_in_t, f_nat)


def _ffn_up_kernel(h_ref, wg_ref, wu_ref, cw_ref, o_ref, gs, wgb, wub, *, tm, tiles_per_seq):
    i = pl.program_id(1)

    @pl.when(i == 0)
    def _():
        wgb[...] = wg_ref[...].astype(BF16)
        wub[...] = wu_ref[...].astype(BF16)

    @pl.when(i % tiles_per_seq == 0)
    def _():
        gs[0:SUBLANES, :] = jnp.zeros((SUBLANES, gs.shape[1]), F32)

    h = h_ref[...]
    gs[SUBLANES:, :] = jnp.dot(h, wgb[...], preferred_element_type=F32)
    u = jnp.dot(h, wub[...], preferred_element_type=F32)
    conv = (cw_ref[0:1, :] * gs[SUBLANES - 2:SUBLANES - 2 + tm, :]
            + cw_ref[1:2, :] * gs[SUBLANES - 1:SUBLANES - 1 + tm, :]
            + cw_ref[2:3, :] * gs[SUBLANES:, :])
    o_ref[...] = (conv * jax.nn.sigmoid(conv) * u).astype(o_ref.dtype)
    gs[0:SUBLANES, :] = gs[tm:tm + SUBLANES, :]


def ffn_up(h, w_up, conv_w, layer, d_ff, seq):
    m, d = h.shape
    tm, tn = 2048, 512
    nj = d_ff // tn
    return pl.pallas_call(
        functools.partial(_ffn_up_kernel, tm=tm, tiles_per_seq=seq // tm),
        grid=(nj, m // tm),
        in_specs=[pl.BlockSpec((tm, d), lambda j, i: (i, 0)),
                  pl.BlockSpec((None, d, tn), lambda j, i: (layer, 0, j)),
                  pl.BlockSpec((None, d, tn), lambda j, i: (layer, 0, nj + j)),
                  pl.BlockSpec((None, FFN_CONV_WIDTH, tn), lambda j, i: (layer, 0, j))],
        out_specs=pl.BlockSpec((tm, tn), lambda j, i: (i, j)),
        out_shape=jax.ShapeDtypeStruct((m, d_ff), BF16),
        scratch_shapes=[pltpu.VMEM((tm + SUBLANES, tn), F32),
                        pltpu.VMEM((d, tn), BF16), pltpu.VMEM((d, tn), BF16)],
        compiler_params=_cparams(("parallel", "arbitrary")),
        name="ffn_up",
    )(h, w_up, w_up, conv_w)


def kernel(x, positions, norm_mix, norm_ffn, norm_final, ev_w_in, ev_conv_w, ev_conv_b, ev_ln_g,
           ev_ln_b, ev_w_out, od_w_in, od_b_f, od_w_out, ffn_w_up, ffn_conv_w, ffn_w_down):
    batch, seq, d = x.shape
    depth = norm_mix.shape[0]
    m = batch * seq
    c_b = ev_conv_w.shape[2]
    n_heads_a = (ev_w_in.shape[2] - 2 * c_b) // (3 * HEAD_DIM)
    n_heads_c = od_b_f.shape[1]
    d_a = n_heads_a * HEAD_DIM
    d_c = n_heads_c * HEAD_DIM
    d_ff = ffn_w_down.shape[1]

    xf = x.reshape(m, d)
    tabs = rope_tables(positions)
    od_w_in_t = jnp.swapaxes(od_w_in, 1, 2)

    for l in range(depth):
        h = rmsnorm(xf, norm_mix[l], BF16)
        if l % 2 == 0:
            e = l // 2
            y_a = even_attention(h, ev_w_in, e, tabs, batch, seq, n_heads_a)
            u = matmul([h], ev_w_in, e, 2 * c_b, F32, tm=1024, tn=1024, col0=3 * d_a)
            y_b = conv_module(u, 0, c_b, ev_conv_w[e], ev_conv_b[e], ev_ln_g[e], ev_ln_b[e],
                              batch, seq)
            xf, h = matmul([y_a, y_b], ev_w_out, e, d, F32, res=xf, tm=512, tn=d, norm_g=norm_ffn[l])
        else:
            o = l // 2
            b_f = jnp.pad(od_b_f[o], (0, LANES - n_heads_c)).reshape(1, LANES)
            f_nat = forget_gates(h, od_w_in_t, o, 3 * d_c, n_heads_c, b_f, batch, seq)
            y = odd_mixer(h, od_w_in_t, o, f_nat, batch, seq, n_heads_c)
            xf, h = matmul([y], od_w_out, o, d, F32, res=xf, tm=512, tn=d, norm_g=norm_ffn[l])
        act = ffn_up(h, ffn_w_up, ffn_conv_w, l, d_ff, seq)
        xf = matmul([act], ffn_w_down, l, d, F32, res=xf, tm=512, tn=512)
    return rmsnorm(xf, norm_final, F32).reshape(batch, seq, d)
```

```python
import functools

import jax
import jax.numpy as jnp
from jax import lax
from jax.experimental import pallas as pl
from jax.experimental.pallas import tpu as pltpu

F32 = jnp.float32
BF16 = jnp.bfloat16

HEAD_DIM = 128
ROT_DIM = 32
ROPE_THETA = 500000.0
CONV_B_WIDTH = 31
CONV_HALO = 32
FFN_CONV_WIDTH = 3
EPS = 1e-6
NEG = -1e30
LOG2E = 1.4426950408889634
DIL_W = 128
LANES = 128
SUBLANES = 8
VMEM_LIMIT = 56 * 1024 * 1024

_NT = (((1,), (1,)), ((), ()))


def _cparams(sem):
    return pltpu.CompilerParams(dimension_semantics=sem, vmem_limit_bytes=VMEM_LIMIT)


def _rmsnorm_kernel(x_ref, g_ref, o_ref):
    x = x_ref[...]
    ms = jnp.mean(x * x, axis=-1, keepdims=True)
    o_ref[...] = (x * lax.rsqrt(ms + EPS) * g_ref[...]).astype(o_ref.dtype)


def rmsnorm(x, g, out_dtype):
    m, d = x.shape
    tm = 512
    return pl.pallas_call(
        _rmsnorm_kernel,
        grid=(m // tm,),
        in_specs=[pl.BlockSpec((tm, d), lambda i: (i, 0)),
                  pl.BlockSpec((1, d), lambda i: (0, 0))],
        out_specs=pl.BlockSpec((tm, d), lambda i: (i, 0)),
        out_shape=jax.ShapeDtypeStruct((m, d), out_dtype),
        compiler_params=_cparams(("parallel",)),
        name="rmsnorm",
    )(x, g.reshape(1, d))


def _mm_kernel(*refs, n_ops, has_res, has_norm):
    a_refs = refs[:n_ops]
    w_refs = refs[n_ops:2 * n_ops]
    res_ref = refs[2 * n_ops] if has_res else None
    g_ref = refs[2 * n_ops + has_res] if has_norm else None
    o_ref = refs[2 * n_ops + has_res + has_norm]
    hn_ref = refs[2 * n_ops + has_res + has_norm + 1] if has_norm else None
    wb_refs = refs[2 * n_ops + has_res + 2 * has_norm + 1:]

    @pl.when(pl.program_id(1) == 0)
    def _():
        for w_ref, wb_ref in zip(w_refs, wb_refs):
            wb_ref[...] = w_ref[...].astype(BF16)

    acc = jnp.dot(a_refs[0][...], wb_refs[0][...], preferred_element_type=F32)
    for a_ref, wb_ref in zip(a_refs[1:], wb_refs[1:]):
        acc = acc + jnp.dot(a_ref[...], wb_ref[...], preferred_element_type=F32)
    if has_res:
        acc = acc + res_ref[...]
    o_ref[...] = acc.astype(o_ref.dtype)
    if has_norm:
        ms = jnp.mean(acc * acc, axis=-1, keepdims=True)
        hn_ref[...] = (acc * lax.rsqrt(ms + EPS) * g_ref[...]).astype(hn_ref.dtype)


def matmul(a_list, w, layer, n_cols, out_dtype, *, res=None, tm, tn, col0=0, norm_g=None):
    m = a_list[0].shape[0]
    kk = a_list[0].shape[1]
    n_ops = len(a_list)
    assert all(a.shape == (m, kk) for a in a_list) and w.shape[1] == n_ops * kk
    assert m % tm == 0 and n_cols % tn == 0 and col0 % tn == 0
    jb = col0 // tn
    in_specs = [pl.BlockSpec((tm, kk), lambda j, i: (i, 0)) for _ in a_list]
    has_norm = norm_g is not None
    assert not has_norm or tn == n_cols
    wmode = dict(pipeline_mode=pl.Buffered(1)) if has_norm else {}
    in_specs += [pl.BlockSpec((None, kk, tn), functools.partial(lambda j, i, r: (layer, r, jb + j), r=r), **wmode)
                 for r in range(n_ops)]
    args = list(a_list) + [w] * n_ops
    if res is not None:
        in_specs.append(pl.BlockSpec((tm, tn), lambda j, i: (i, j)))
        args.append(res)
    out_spec = pl.BlockSpec((tm, tn), lambda j, i: (i, j))
    out_specs, out_shape = out_spec, jax.ShapeDtypeStruct((m, n_cols), out_dtype)
    if has_norm:
        in_specs.append(pl.BlockSpec((1, tn), lambda j, i: (0, 0)))
        args.append(norm_g.reshape(1, n_cols))
        out_specs, out_shape = [out_spec, out_spec], [out_shape, jax.ShapeDtypeStruct((m, n_cols), BF16)]
    kern = functools.partial(_mm_kernel, n_ops=n_ops, has_res=res is not None, has_norm=has_norm)
    return pl.pallas_call(
        kern,
        grid=(n_cols // tn, m // tm),
        in_specs=in_specs,
        out_specs=out_specs,
        out_shape=out_shape,
        scratch_shapes=[pltpu.VMEM((kk, tn), BF16) for _ in range(n_ops)],
        compiler_params=_cparams(("parallel", "arbitrary")),
        name="matmul",
    )(*args)


def _rope_table_kernel(pos_ref, inv_ref, c_ref, sa_ref, sb_ref):
    ang = pos_ref[...].astype(F32) * inv_ref[...]
    lane = lax.broadcasted_iota(jnp.int32, ang.shape, 1)
    cos = jnp.cos(ang)
    sin = jnp.sin(ang)
    half = ROT_DIM // 2
    c_ref[...] = jnp.where(lane < ROT_DIM, cos, 1.0)
    sa_ref[...] = jnp.where(lane < half, -sin, 0.0)
    sb_ref[...] = jnp.where(lane < ROT_DIM, jnp.where(lane >= half, sin, 0.0), 0.0)


def rope_tables(positions):
    m = positions.size
    half = ROT_DIM // 2
    inv = jnp.power(jnp.float32(ROPE_THETA), -jnp.arange(half, dtype=F32) * (2.0 / ROT_DIM))
    inv_lane = jnp.concatenate([inv, inv, jnp.zeros((LANES - ROT_DIM,), F32)]).reshape(1, LANES)
    tm = 1024
    shp = jax.ShapeDtypeStruct((m, LANES), F32)
    spec = pl.BlockSpec((tm, LANES), lambda i: (i, 0))
    return pl.pallas_call(
        _rope_table_kernel,
        grid=(m // tm,),
        in_specs=[pl.BlockSpec((tm, 1), lambda i: (i, 0)),
                  pl.BlockSpec((1, LANES), lambda i: (0, 0))],
        out_specs=[spec, spec, spec],
        out_shape=[shp, shp, shp],
        compiler_params=_cparams(("parallel",)),
        name="rope_tables",
    )(positions.reshape(m, 1), inv_lane)


def _rope(x, c, sa, sb):
    half = ROT_DIM // 2
    return x * c + pltpu.roll(x, LANES - half, 1) * sa + pltpu.roll(x, half, 1) * sb


def _dilated_core(qr, kr, v_ref, o_ref, o1, o2, o3, l1, l2, l3, s_scr, p_scr, *, seq):
    blocks = []
    for dst, dil in ((0, 1), (1, 4), (2, 16)):
        for r in range(dil):
            for n in range(seq // (dil * DIL_W)):
                q0 = r + n * dil * DIL_W
                blocks.append((dst, q0, q0 - dil * DIL_W, 2 * DIL_W, dil) if n else
                              (dst, q0, q0, DIL_W, dil))
    outs = ((o1, l1), (o2, l2), (o3, l3))

    def sl(start, n, stride):
        return pl.ds(start, n) if stride == 1 else pl.ds(start, n, stride=stride)

    i = lax.broadcasted_iota(jnp.int32, (DIL_W, 2 * DIL_W), 0)
    j = lax.broadcasted_iota(jnp.int32, (DIL_W, 2 * DIL_W), 1)
    dist = DIL_W + i - j
    band2 = (dist >= 0) & (dist <= DIL_W)
    band1 = (lax.broadcasted_iota(jnp.int32, (DIL_W, DIL_W), 0)
             >= lax.broadcasted_iota(jnp.int32, (DIL_W, DIL_W), 1))

    def scores(t):
        _, q0, k0, nk, st = blocks[t]
        qs = qr[sl(q0, DIL_W, st), :].astype(BF16)
        ks = kr[sl(k0, nk, st), :].astype(BF16)
        s_scr[t % 2, :, :nk] = lax.dot_general(qs, ks, _NT, preferred_element_type=F32)

    def softmax(t):
        nk = blocks[t][3]
        s = jnp.where(band1 if nk == DIL_W else band2, s_scr[t % 2, :, :nk], NEG)
        m = jnp.max(s, axis=-1, keepdims=True)
        p = jnp.exp(s - m)
        p_scr[t % 2, :, :nk] = p.astype(BF16)
        return m, jnp.sum(p, axis=-1, keepdims=True)

    def finish(t, m, den):
        dst, q0, k0, nk, st = blocks[t]
        vs = v_ref[sl(k0, nk, st), :].astype(BF16)
        o = jnp.dot(p_scr[t % 2, :, :nk], vs, preferred_element_type=F32) * (1.0 / den)
        o_dst, l_dst = outs[dst]
        o_dst[sl(q0, DIL_W, st), :] = o
        l_dst[sl(q0, DIL_W, st), :] = jnp.broadcast_to(m + jnp.log(den), (DIL_W, LANES))

    nblocks = len(blocks)
    scores(0)
    scores(1)
    stats = softmax(0)
    for t in range(nblocks):
        if t + 2 < nblocks:
            scores(t + 2)
        finish(t, *stats)
        if t + 1 < nblocks:
            stats = softmax(t + 1)

    rows = 256

    def merge_body(t, _):
        rs = pl.ds(pl.multiple_of(t * rows, rows), rows)
        la, lb, lc = l1[rs, :], l2[rs, :], l3[rs, :]
        m = jnp.maximum(jnp.maximum(la, lb), lc)
        ea, eb, ec = jnp.exp(la - m), jnp.exp(lb - m), jnp.exp(lc - m)
        inv = 1.0 / (ea + eb + ec)
        out = (ea * inv) * o1[rs, :] + (eb * inv) * o2[rs, :] + (ec * inv) * o3[rs, :]
        o_ref[rs, :] = out.astype(o_ref.dtype)
        return 0
    lax.fori_loop(0, seq // rows, merge_body, 0)


def _even_attn_kernel(hn_ref, wq_ref, wk_ref, wv_ref, c_ref, sa_ref, sb_ref, o_ref, wb, r_scr,
                      qr, kr, vr, o1, o2, o3, l1, l2, l3, s_scr, p_scr, *, seq):
    @pl.when(pl.program_id(0) == 0)
    def _():
        r_scr[...] = jnp.zeros(r_scr.shape, F32)

    c, sa, sb = c_ref[...], sa_ref[...], sb_ref[...]
    qr[...] = _rope(r_scr[:, :HEAD_DIM], c, sa, sb) * (HEAD_DIM ** -0.5)
    kr[...] = _rope(r_scr[:, HEAD_DIM:2 * HEAD_DIM], c, sa, sb)
    vr[...] = r_scr[:, 2 * HEAD_DIM:]

    wb[:, :HEAD_DIM] = wq_ref[...].astype(BF16)
    wb[:, HEAD_DIM:2 * HEAD_DIM] = wk_ref[...].astype(BF16)
    wb[:, 2 * HEAD_DIM:] = wv_ref[...].astype(BF16)
    r_scr[...] = jnp.dot(hn_ref[...], wb[...], preferred_element_type=F32)

    _dilated_core(qr, kr, vr, o_ref, o1, o2, o3, l1, l2, l3, s_scr, p_scr, seq=seq)


def even_attention(hn, w_in, layer, tabs, batch, seq, n_heads):
    m, d = hn.shape
    n = batch * n_heads
    proj = lambda t: jnp.minimum(t, n - 1)
    attn = lambda t: jnp.maximum(t - 1, 0)
    blk = (seq, HEAD_DIM)
    wspec = lambda g: pl.BlockSpec((None, d, HEAD_DIM), lambda t: (layer, 0, g * n_heads + proj(t) % n_heads))
    tab_spec = pl.BlockSpec(blk, lambda t: (attn(t) // n_heads, 0))
    scratch = [pltpu.VMEM((d, 3 * HEAD_DIM), BF16), pltpu.VMEM((seq, 3 * HEAD_DIM), F32)]
    scratch += [pltpu.VMEM(blk, F32) for _ in range(9)]
    scratch += [pltpu.VMEM((2, DIL_W, 2 * DIL_W), F32), pltpu.VMEM((2, DIL_W, 2 * DIL_W), BF16)]
    return pl.pallas_call(
        functools.partial(_even_attn_kernel, seq=seq),
        grid=(n + 1,),
        in_specs=[pl.BlockSpec((seq, d), lambda t: (proj(t) // n_heads, 0)),
                  wspec(0), wspec(1), wspec(2), tab_spec, tab_spec, tab_spec],
        out_specs=pl.BlockSpec(blk, lambda t: (attn(t) // n_heads, attn(t) % n_heads)),
        out_shape=jax.ShapeDtypeStruct((m, n_heads * HEAD_DIM), BF16),
        scratch_shapes=scratch,
        compiler_params=_cparams(("arbitrary",)),
        name="even_attention",
    )(hn, w_in, w_in, w_in, *tabs)


def _conv_core(halo, cur, cw_ref, cb_ref, lng_ref, lnb_ref, o_ref, gs, cs, *, ts, ch):
    for b in range(SUBLANES):
        gs[b, b:b + CONV_HALO, :] = halo
        gs[b, b + CONV_HALO:b + CONV_HALO + ts, :] = cur

    rb = 64
    first = CONV_HALO - (CONV_B_WIDTH - 1)
    for c in range(ch // LANES):
        cl = slice(c * LANES, (c + 1) * LANES)
        for r in range(ts // rb):
            acc = jnp.zeros((rb, LANES), F32)
            for k in range(CONV_B_WIDTH):
                b = -(first + k) % SUBLANES
                start = r * rb + first + k + b
                acc = acc + cw_ref[k:k + 1, cl] * gs[b, start:start + rb, cl]
            cs[r * rb:(r + 1) * rb, cl] = acc + cb_ref[:, cl]

    lrows = 32

    def ln_body(t, _):
        rs = pl.ds(pl.multiple_of(t * lrows, lrows), lrows)
        x = cs[rs, :]
        mu = jnp.mean(x, axis=-1, keepdims=True)
        xc = x - mu
        var = jnp.mean(xc * xc, axis=-1, keepdims=True)
        y = xc * lax.rsqrt(var + EPS) * lng_ref[...] + lnb_ref[...]
        o_ref[rs, :] = (y * jax.nn.sigmoid(y)).astype(o_ref.dtype)
        return 0
    lax.fori_loop(0, ts // lrows, ln_body, 0, unroll=4)


def _conv_mixer_kernel(hn_ref, wa_ref, wg_ref, cw_ref, cb_ref, lng_ref, lnb_ref, o_ref,
                       wb, r_scr, hs, gs, cs, *, ts, ch, tiles_per_seq):
    t = pl.program_id(0)

    @pl.when(t == 0)
    def _():
        wb[:, :ch] = wa_ref[...].astype(BF16)
        wb[:, ch:] = wg_ref[...].astype(BF16)
        r_scr[...] = jnp.zeros(r_scr.shape, F32)
        hs[...] = jnp.zeros(hs.shape, F32)

    cur = r_scr[:, :ch] * jax.nn.sigmoid(r_scr[:, ch:])
    first_of_seq = jnp.maximum(t - 1, 0) % tiles_per_seq == 0
    halo = jnp.where(first_of_seq, 0.0, hs[...])
    hs[...] = cur[ts - CONV_HALO:, :]

    r_scr[...] = jnp.dot(hn_ref[...], wb[...], preferred_element_type=F32)

    _conv_core(halo, cur, cw_ref, cb_ref, lng_ref, lnb_ref, o_ref, gs, cs, ts=ts, ch=ch)


def conv_mixer(hn, w_in, layer, col0, ch, conv_w, conv_b, ln_g, ln_b, seq):
    m, d = hn.shape
    ts = 256
    n = m // ts
    assert col0 % ch == 0
    proj = lambda t: jnp.minimum(t, n - 1)
    attn = lambda t: jnp.maximum(t - 1, 0)
    wspec = lambda g: pl.BlockSpec((None, d, ch), lambda t: (layer, 0, col0 // ch + g),
                                   pipeline_mode=pl.Buffered(1))
    vec = lambda rows: pl.BlockSpec((rows, ch), lambda t: (0, 0))
    return pl.pallas_call(
        functools.partial(_conv_mixer_kernel, ts=ts, ch=ch, tiles_per_seq=seq // ts),
        grid=(n + 1,),
        in_specs=[pl.BlockSpec((ts, d), lambda t: (proj(t), 0)), wspec(0), wspec(1),
                  vec(CONV_B_WIDTH), vec(1), vec(1), vec(1)],
        out_specs=pl.BlockSpec((ts, ch), lambda t: (attn(t), 0)),
        out_shape=jax.ShapeDtypeStruct((m, ch), BF16),
        scratch_shapes=[pltpu.VMEM((d, 2 * ch), BF16), pltpu.VMEM((ts, 2 * ch), F32),
                        pltpu.VMEM((CONV_HALO, ch), F32),
                        pltpu.VMEM((SUBLANES, CONV_HALO + ts + SUBLANES, ch), F32),
                        pltpu.VMEM((ts, ch), F32)],
        compiler_params=_cparams(("arbitrary",)),
        name="conv_mixer",
    )(hn, w_in, w_in, conv_w, conv_b.reshape(1, ch), ln_g.reshape(1, ch), ln_b.reshape(1, ch))


def _gates_kernel(h_ref, w_ref, b_ref, f_ref, *, seq):
    w_t = w_ref[...].astype(BF16)
    w_t = jnp.concatenate([w_t, jnp.zeros((LANES - w_t.shape[0], w_t.shape[1]), BF16)], axis=0)
    fl = lax.dot_general(h_ref[...], w_t, _NT, preferred_element_type=F32) + b_ref[...]
    logf = jnp.minimum(fl, 0.0) - jnp.log1p(jnp.exp(-jnp.abs(fl)))
    blk = 128
    ii = lax.broadcasted_iota(jnp.int32, (blk, blk), 0)
    jj = lax.broadcasted_iota(jnp.int32, (blk, blk), 1)
    tri = jnp.where(jj <= ii, 1.0, 0.0).astype(BF16)
    carry = jnp.zeros((1, LANES), F32)
    for t in range(seq // blk):
        x = logf[t * blk:(t + 1) * blk, :]
        hi = x.astype(BF16)
        r1 = x - hi.astype(F32)
        mid = r1.astype(BF16)
        lo = (r1 - mid.astype(F32)).astype(BF16)
        cum = (jnp.dot(tri, hi, preferred_element_type=F32)
               + jnp.dot(tri, mid, preferred_element_type=F32)
               + jnp.dot(tri, lo, preferred_element_type=F32)) + carry
        f_ref[t * blk:(t + 1) * blk, :] = cum
        carry = cum[blk - 1:blk, :]


def forget_gates(h, w_in_t, layer, row0, n_heads, b_f, batch, seq):
    m, d = h.shape
    assert row0 % n_heads == 0
    return pl.pallas_call(
        functools.partial(_gates_kernel, seq=seq),
        grid=(batch,),
        in_specs=[pl.BlockSpec((seq, d), lambda b: (b, 0)),
                  pl.BlockSpec((None, n_heads, d), lambda b: (layer, row0 // n_heads, 0)),
                  pl.BlockSpec((1, LANES), lambda b: (0, 0))],
        out_specs=pl.BlockSpec((seq, LANES), lambda b: (b, 0)),
        out_shape=jax.ShapeDtypeStruct((m, LANES), F32),
        compiler_params=_cparams(("parallel",)),
        name="forget_gates",
    )(h, w_in_t, b_f)


def _fox_bias_columns(f_col, qa, ka, *, seq, tq):
    nblk = seq // tq
    lane = lax.broadcasted_iota(jnp.int32, (seq, LANES), 1)
    f2 = f_col * LOG2E
    base = [f2[b * tq:b * tq + 1, :] for b in range(nblk)]
    d = jnp.concatenate([f2[b * tq:(b + 1) * tq, :] - base[b] for b in range(nblk)], axis=0)
    hi = d.astype(BF16).astype(F32)
    r1 = d - hi
    mid = r1.astype(BF16).astype(F32)
    lo = (r1 - mid).astype(BF16).astype(F32)
    odd = (lane & 1) == 1
    term = jnp.where(lane < 2, hi, jnp.where(lane < 4, mid, lo))
    live = lane < 6
    qa[:, HEAD_DIM:] = jnp.where(live, jnp.where(odd, 1.0, term), 0.0).astype(BF16)
    ka[:, HEAD_DIM:] = jnp.where(live, jnp.where(odd, -term, 1.0), 0.0).astype(BF16)
    return base


def _fox_head(qa, ka, vt, base, s_scr, p_scr, store, *, seq, tq):
    nblk = seq // tq
    row = lax.broadcasted_iota(jnp.int32, (tq, tq), 0)
    col = lax.broadcasted_iota(jnp.int32, (tq, tq), 1)
    pairs = [(qi, kj) for qi in range(nblk) for kj in range(qi + 1)]

    def scores(qi, kj):
        return lax.dot_general(ka[kj * tq:(kj + 1) * tq, :], qa[qi * tq:(qi + 1) * tq, :], _NT,
                               preferred_element_type=F32)

    def softmax(t, m, l):
        qi, kj = pairs[t]
        s = s_scr[t % 2]
        if kj == qi:
            s = jnp.where(row <= col, s, NEG)
        blk_max = jnp.max(s, axis=0, keepdims=True)
        if kj != qi:
            blk_max = blk_max + (base[qi] - base[kj])
        if kj == 0:
            m_new = blk_max
            alpha = None
        else:
            m_new = jnp.maximum(m, blk_max)
            alpha = jnp.exp2(m - m_new)
        p = jnp.exp2(s - (m_new if kj == qi else m_new - (base[qi] - base[kj])))
        p_scr[t % 2] = p.astype(BF16)
        psum = jnp.sum(p, axis=0, keepdims=True)
        return m_new, (psum if kj == 0 else alpha * l + psum), alpha

    n = len(pairs)
    s_scr[0] = scores(*pairs[0])
    if n > 1:
        s_scr[1] = scores(*pairs[1])
    m, l, alpha = softmax(0, None, None)
    acc = None
    for t, (qi, kj) in enumerate(pairs):
        if t + 2 < n:
            s_scr[t % 2] = scores(*pairs[t + 2])
        pv = jnp.dot(vt[:, kj * tq:(kj + 1) * tq], p_scr[t % 2],
                     preferred_element_type=F32)
        acc = pv if kj == 0 else alpha * acc + pv
        l_t = l
        if t + 1 < n:
            m, l, alpha = softmax(t + 1, m, l)
        if kj == qi:
            store(qi, (acc * (1.0 / l_t)).T)


def _odd_mixer_kernel(hn_ref, wq_ref, wk_ref, wv_ref, fn_ref, o_ref, wb, r_scr, qa, ka, vt, s_scr, p_scr,
                      *, seq, tq, scale, pairs_per_batch):
    t = pl.program_id(0)
    d2 = 2 * HEAD_DIM

    @pl.when(t == 0)
    def _():
        r_scr[...] = jnp.zeros(r_scr.shape, F32)

    for hh in range(2):
        cs = slice(hh * HEAD_DIM, (hh + 1) * HEAD_DIM)
        qa[hh, :, :HEAD_DIM] = (r_scr[:, cs] * scale).astype(BF16)
        ka[hh, :, :HEAD_DIM] = r_scr[:, d2 + hh * HEAD_DIM:d2 + (hh + 1) * HEAD_DIM].astype(BF16)
        vt[hh] = r_scr[:, 2 * d2 + hh * HEAD_DIM:2 * d2 + (hh + 1) * HEAD_DIM].T.astype(BF16)

    wb[0:d2, :] = wq_ref[...].astype(BF16)
    wb[d2:2 * d2, :] = wk_ref[...].astype(BF16)
    wb[2 * d2:3 * d2, :] = wv_ref[...].astype(BF16)
    r_scr[...] = lax.dot_general(hn_ref[...], wb[...], _NT, preferred_element_type=F32)

    pair = jnp.maximum(t - 1, 0) % pairs_per_batch
    lane = lax.broadcasted_iota(jnp.int32, (seq, LANES), 1)
    for hh in range(2):
        f_col = jnp.sum(jnp.where(lane == 2 * pair + hh, fn_ref[...], 0.0), axis=1, keepdims=True)
        base = _fox_bias_columns(f_col, qa.at[hh], ka.at[hh], seq=seq, tq=tq)

        def store(qi, blk, hh=hh):
            o_ref[qi * tq:(qi + 1) * tq, hh * HEAD_DIM:(hh + 1) * HEAD_DIM] = blk.astype(o_ref.dtype)
        _fox_head(qa.at[hh], ka.at[hh], vt.at[hh], base, s_scr.at[hh], p_scr.at[hh], store, seq=seq, tq=tq)


def odd_mixer(hn, w_in_t, layer, f_nat, batch, seq, n_heads):
    m, d = hn.shape
    tq = 256
    d2 = 2 * HEAD_DIM
    ppb = n_heads // 2
    n_pairs = batch * ppb
    proj = lambda t: jnp.minimum(t, n_pairs - 1)
    attn = lambda t: jnp.maximum(t - 1, 0)
    wspec = lambda g: pl.BlockSpec((None, d2, d), lambda t: (layer, g * ppb + proj(t) % ppb, 0))
    return pl.pallas_call(
        functools.partial(_odd_mixer_kernel, seq=seq, tq=tq, scale=LOG2E * HEAD_DIM ** -0.5,
                          pairs_per_batch=ppb),
        grid=(n_pairs + 1,),
        in_specs=[pl.BlockSpec((seq, d), lambda t: (proj(t) // ppb, 0)),
                  wspec(0), wspec(1), wspec(2),
                  pl.BlockSpec((seq, LANES), lambda t: (attn(t) // ppb, 0))],
        out_specs=pl.BlockSpec((seq, d2), lambda t: (attn(t) // ppb, attn(t) % ppb)),
        out_shape=jax.ShapeDtypeStruct((m, n_heads * HEAD_DIM), BF16),
        scratch_shapes=[pltpu.VMEM((3 * d2, d), BF16), pltpu.VMEM((seq, 3 * d2), F32),
                        pltpu.VMEM((2, seq, d2), BF16), pltpu.VMEM((2, seq, d2), BF16),
                        pltpu.VMEM((2, HEAD_DIM, seq), BF16),
                        pltpu.VMEM((2, 2, tq, tq), F32), pltpu.VMEM((2, 2, tq, tq), BF16)],
        compiler_params=_cparams(("arbitrary",)),
        name="odd_mixer",
    )(hn, w_in_t, w_in_t, w_in_t, f_nat)


def _ffn_up_kernel(h_ref, wg_ref, wu_ref, cw_ref, o_ref, gs, wgb, wub, *, tm, tiles_per_seq):
    i = pl.program_id(1)

    @pl.when(i == 0)
    def _():
        wgb[...] = wg_ref[...].astype(BF16)
        wub[...] = wu_ref[...].astype(BF16)

    @pl.when(i % tiles_per_seq == 0)
    def _():
        gs[0:SUBLANES, :] = jnp.zeros((SUBLANES, gs.shape[1]), F32)

    h = h_ref[...]
    gs[SUBLANES:, :] = jnp.dot(h, wgb[...], preferred_element_type=F32)
    u = jnp.dot(h, wub[...], preferred_element_type=F32)
    conv = (cw_ref[0:1, :] * gs[SUBLANES - 2:SUBLANES - 2 + tm, :]
            + cw_ref[1:2, :] * gs[SUBLANES - 1:SUBLANES - 1 + tm, :]
            + cw_ref[2:3, :] * gs[SUBLANES:, :])
    o_ref[...] = (conv * jax.nn.sigmoid(conv) * u).astype(o_ref.dtype)
    gs[0:SUBLANES, :] = gs[tm:tm + SUBLANES, :]


def ffn_up(h, w_up, conv_w, layer, d_ff, seq):
    m, d = h.shape
    tm, tn = 2048, 512
    nj = d_ff // tn
    return pl.pallas_call(
        functools.partial(_ffn_up_kernel, tm=tm, tiles_per_seq=seq // tm),
        grid=(nj, m // tm),
        in_specs=[pl.BlockSpec((tm, d), lambda j, i: (i, 0)),
                  pl.BlockSpec((None, d, tn), lambda j, i: (layer, 0, j)),
                  pl.BlockSpec((None, d, tn), lambda j, i: (layer, 0, nj + j)),
                  pl.BlockSpec((None, FFN_CONV_WIDTH, tn), lambda j, i: (layer, 0, j))],
        out_specs=pl.BlockSpec((tm, tn), lambda j, i: (i, j)),
        out_shape=jax.ShapeDtypeStruct((m, d_ff), BF16),
        scratch_shapes=[pltpu.VMEM((tm + SUBLANES, tn), F32),
                        pltpu.VMEM((d, tn), BF16), pltpu.VMEM((d, tn), BF16)],
        compiler_params=_cparams(("parallel", "arbitrary")),
        name="ffn_up",
    )(h, w_up, w_up, conv_w)


def kernel(x, positions, norm_mix, norm_ffn, norm_final, ev_w_in, ev_conv_w, ev_conv_b, ev_ln_g,
           ev_ln_b, ev_w_out, od_w_in, od_b_f, od_w_out, ffn_w_up, ffn_conv_w, ffn_w_down):
    batch, seq, d = x.shape
    depth = norm_mix.shape[0]
    m = batch * seq
    c_b = ev_conv_w.shape[2]
    n_heads_a = (ev_w_in.shape[2] - 2 * c_b) // (3 * HEAD_DIM)
    n_heads_c = od_b_f.shape[1]
    d_a = n_heads_a * HEAD_DIM
    d_c = n_heads_c * HEAD_DIM
    d_ff = ffn_w_down.shape[1]

    xf = x.reshape(m, d)
    tabs = rope_tables(positions)
    od_w_in_t = jnp.swapaxes(od_w_in, 1, 2)

    for l in range(depth):
        h = rmsnorm(xf, norm_mix[l], BF16)
        if l % 2 == 0:
            e = l // 2
            y_a = even_attention(h, ev_w_in, e, tabs, batch, seq, n_heads_a)
            y_b = conv_mixer(h, ev_w_in, e, 3 * d_a, c_b, ev_conv_w[e], ev_conv_b[e], ev_ln_g[e], ev_ln_b[e],
                             seq)
            xf, h = matmul([y_a, y_b], ev_w_out, e, d, F32, res=xf, tm=512, tn=d, norm_g=norm_ffn[l])
        else:
            o = l // 2
            b_f = jnp.pad(od_b_f[o], (0, LANES - n_heads_c)).reshape(1, LANES)
            f_nat = forget_gates(h, od_w_in_t, o, 3 * d_c, n_heads_c, b_f, batch, seq)
            y = odd_mixer(h, od_w_in_t, o, f_nat, batch, seq, n_heads_c)
            xf, h = matmul([y], od_w_out, o, d, F32, res=xf, tm=512, tn=d, norm_g=norm_ffn[l])
        act = ffn_up(h, ffn_w_up, ffn_conv_w, l, d_ff, seq)
        xf = matmul([act], ffn_w_down, l, d, F32, res=xf, tm=512, tn=512)
    return rmsnorm(xf, norm_final, F32).reshape(batch, seq, d)
```

```python
import functools

import jax
import jax.numpy as jnp
from jax import lax
from jax.experimental import pallas as pl
from jax.experimental.pallas import tpu as pltpu

F32 = jnp.float32
BF16 = jnp.bfloat16

HEAD_DIM = 128
ROT_DIM = 32
ROPE_THETA = 500000.0
CONV_B_WIDTH = 31
CONV_HALO = 32
FFN_CONV_WIDTH = 3
EPS = 1e-6
NEG = -1e30
LOG2E = 1.4426950408889634
DIL_W = 128
LANES = 128
SUBLANES = 8
VMEM_LIMIT = 56 * 1024 * 1024

_NT = (((1,), (1,)), ((), ()))


def _cparams(sem):
    return pltpu.CompilerParams(dimension_semantics=sem, vmem_limit_bytes=VMEM_LIMIT)


def _rmsnorm_kernel(x_ref, g_ref, o_ref):
    x = x_ref[...]
    ms = jnp.mean(x * x, axis=-1, keepdims=True)
    o_ref[...] = (x * lax.rsqrt(ms + EPS) * g_ref[...]).astype(o_ref.dtype)


def rmsnorm(x, g, out_dtype):
    m, d = x.shape
    tm = 512
    return pl.pallas_call(
        _rmsnorm_kernel,
        grid=(m // tm,),
        in_specs=[pl.BlockSpec((tm, d), lambda i: (i, 0)),
                  pl.BlockSpec((1, d), lambda i: (0, 0))],
        out_specs=pl.BlockSpec((tm, d), lambda i: (i, 0)),
        out_shape=jax.ShapeDtypeStruct((m, d), out_dtype),
        compiler_params=_cparams(("parallel",)),
        name="rmsnorm",
    )(x, g.reshape(1, d))


def _mm_kernel(*refs, n_ops, has_res, has_norm):
    a_refs = refs[:n_ops]
    w_refs = refs[n_ops:2 * n_ops]
    res_ref = refs[2 * n_ops] if has_res else None
    g_ref = refs[2 * n_ops + has_res] if has_norm else None
    o_ref = refs[2 * n_ops + has_res + has_norm]
    hn_ref = refs[2 * n_ops + has_res + has_norm + 1] if has_norm else None
    wb_refs = refs[2 * n_ops + has_res + 2 * has_norm + 1:]

    @pl.when(pl.program_id(1) == 0)
    def _():
        for w_ref, wb_ref in zip(w_refs, wb_refs):
            wb_ref[...] = w_ref[...].astype(BF16)

    acc = jnp.dot(a_refs[0][...], wb_refs[0][...], preferred_element_type=F32)
    for a_ref, wb_ref in zip(a_refs[1:], wb_refs[1:]):
        acc = acc + jnp.dot(a_ref[...], wb_ref[...], preferred_element_type=F32)
    if has_res:
        acc = acc + res_ref[...]
    o_ref[...] = acc.astype(o_ref.dtype)
    if has_norm:
        ms = jnp.mean(acc * acc, axis=-1, keepdims=True)
        hn_ref[...] = (acc * lax.rsqrt(ms + EPS) * g_ref[...]).astype(hn_ref.dtype)


def matmul(a_list, w, layer, n_cols, out_dtype, *, res=None, tm, tn, col0=0, norm_g=None):
    m = a_list[0].shape[0]
    kk = a_list[0].shape[1]
    n_ops = len(a_list)
    assert all(a.shape == (m, kk) for a in a_list) and w.shape[1] == n_ops * kk
    assert m % tm == 0 and n_cols % tn == 0 and col0 % tn == 0
    jb = col0 // tn
    in_specs = [pl.BlockSpec((tm, kk), lambda j, i: (i, 0)) for _ in a_list]
    has_norm = norm_g is not None
    assert not has_norm or tn == n_cols
    wmode = dict(pipeline_mode=pl.Buffered(1)) if has_norm else {}
    in_specs += [pl.BlockSpec((None, kk, tn), functools.partial(lambda j, i, r: (layer, r, jb + j), r=r), **wmode)
                 for r in range(n_ops)]
    args = list(a_list) + [w] * n_ops
    if res is not None:
        in_specs.append(pl.BlockSpec((tm, tn), lambda j, i: (i, j)))
        args.append(res)
    out_spec = pl.BlockSpec((tm, tn), lambda j, i: (i, j))
    out_specs, out_shape = out_spec, jax.ShapeDtypeStruct((m, n_cols), out_dtype)
    if has_norm:
        in_specs.append(pl.BlockSpec((1, tn), lambda j, i: (0, 0)))
        args.append(norm_g.reshape(1, n_cols))
        out_specs, out_shape = [out_spec, out_spec], [out_shape, jax.ShapeDtypeStruct((m, n_cols), BF16)]
    kern = functools.partial(_mm_kernel, n_ops=n_ops, has_res=res is not None, has_norm=has_norm)
    return pl.pallas_call(
        kern,
        grid=(n_cols // tn, m // tm),
        in_specs=in_specs,
        out_specs=out_specs,
        out_shape=out_shape,
        scratch_shapes=[pltpu.VMEM((kk, tn), BF16) for _ in range(n_ops)],
        compiler_params=_cparams(("parallel", "arbitrary")),
        name="matmul",
    )(*args)


def _rope_table_kernel(pos_ref, inv_ref, c_ref, sa_ref, sb_ref):
    ang = pos_ref[...].astype(F32) * inv_ref[...]
    lane = lax.broadcasted_iota(jnp.int32, ang.shape, 1)
    cos = jnp.cos(ang)
    sin = jnp.sin(ang)
    half = ROT_DIM // 2
    c_ref[...] = jnp.where(lane < ROT_DIM, cos, 1.0)
    sa_ref[...] = jnp.where(lane < half, -sin, 0.0)
    sb_ref[...] = jnp.where(lane < ROT_DIM, jnp.where(lane >= half, sin, 0.0), 0.0)


def rope_tables(positions):
    m = positions.size
    half = ROT_DIM // 2
    inv = jnp.power(jnp.float32(ROPE_THETA), -jnp.arange(half, dtype=F32) * (2.0 / ROT_DIM))
    inv_lane = jnp.concatenate([inv, inv, jnp.zeros((LANES - ROT_DIM,), F32)]).reshape(1, LANES)
    tm = 1024
    shp = jax.ShapeDtypeStruct((m, LANES), F32)
    spec = pl.BlockSpec((tm, LANES), lambda i: (i, 0))
    return pl.pallas_call(
        _rope_table_kernel,
        grid=(m // tm,),
        in_specs=[pl.BlockSpec((tm, 1), lambda i: (i, 0)),
                  pl.BlockSpec((1, LANES), lambda i: (0, 0))],
        out_specs=[spec, spec, spec],
        out_shape=[shp, shp, shp],
        compiler_params=_cparams(("parallel",)),
        name="rope_tables",
    )(positions.reshape(m, 1), inv_lane)


def _rope(x, c, sa, sb):
    half = ROT_DIM // 2
    return x * c + pltpu.roll(x, LANES - half, 1) * sa + pltpu.roll(x, half, 1) * sb


def _dilated_core(qr, kr, v_ref, o_ref, o1, o2, o3, l1, l2, l3, s_scr, p_scr, *, seq):
    blocks = []
    for dst, dil in ((0, 1), (1, 4), (2, 16)):
        for r in range(dil):
            for n in range(seq // (dil * DIL_W)):
                q0 = r + n * dil * DIL_W
                blocks.append((dst, q0, q0 - dil * DIL_W, 2 * DIL_W, dil) if n else
                              (dst, q0, q0, DIL_W, dil))
    outs = ((o1, l1), (o2, l2), (o3, l3))

    def sl(start, n, stride):
        return pl.ds(start, n) if stride == 1 else pl.ds(start, n, stride=stride)

    i = lax.broadcasted_iota(jnp.int32, (DIL_W, 2 * DIL_W), 0)
    j = lax.broadcasted_iota(jnp.int32, (DIL_W, 2 * DIL_W), 1)
    dist = DIL_W + i - j
    band2 = (dist >= 0) & (dist <= DIL_W)
    band1 = (lax.broadcasted_iota(jnp.int32, (DIL_W, DIL_W), 0)
             >= lax.broadcasted_iota(jnp.int32, (DIL_W, DIL_W), 1))

    def scores(t):
        _, q0, k0, nk, st = blocks[t]
        qs = qr[sl(q0, DIL_W, st), :].astype(BF16)
        ks = kr[sl(k0, nk, st), :].astype(BF16)
        s_scr[t % 2, :, :nk] = lax.dot_general(qs, ks, _NT, preferred_element_type=F32)

    def softmax(t):
        nk = blocks[t][3]
        s = jnp.where(band1 if nk == DIL_W else band2, s_scr[t % 2, :, :nk], NEG)
        m = jnp.max(s, axis=-1, keepdims=True)
        p = jnp.exp(s - m)
        p_scr[t % 2, :, :nk] = p.astype(BF16)
        return m, jnp.sum(p, axis=-1, keepdims=True)

    def finish(t, m, den):
        dst, q0, k0, nk, st = blocks[t]
        vs = v_ref[sl(k0, nk, st), :].astype(BF16)
        o = jnp.dot(p_scr[t % 2, :, :nk], vs, preferred_element_type=F32) * (1.0 / den)
        o_dst, l_dst = outs[dst]
        o_dst[sl(q0, DIL_W, st), :] = o
        l_dst[sl(q0, DIL_W, st), :] = jnp.broadcast_to(m + jnp.log(den), (DIL_W, LANES))

    nblocks = len(blocks)
    scores(0)
    scores(1)
    stats = softmax(0)
    for t in range(nblocks):
        if t + 2 < nblocks:
            scores(t + 2)
        finish(t, *stats)
        if t + 1 < nblocks:
            stats = softmax(t + 1)

    rows = 256

    def merge_body(t, _):
        rs = pl.ds(pl.multiple_of(t * rows, rows), rows)
        la, lb, lc = l1[rs, :], l2[rs, :], l3[rs, :]
        m = jnp.maximum(jnp.maximum(la, lb), lc)
        ea, eb, ec = jnp.exp(la - m), jnp.exp(lb - m), jnp.exp(lc - m)
        inv = 1.0 / (ea + eb + ec)
        out = (ea * inv) * o1[rs, :] + (eb * inv) * o2[rs, :] + (ec * inv) * o3[rs, :]
        o_ref[rs, :] = out.astype(o_ref.dtype)
        return 0
    lax.fori_loop(0, seq // rows, merge_body, 0)


def _even_attn_kernel(hn_ref, wq_ref, wk_ref, wv_ref, c_ref, sa_ref, sb_ref, o_ref, wb, r_scr,
                      qr, kr, vr, o1, o2, o3, l1, l2, l3, s_scr, p_scr, *, seq):
    @pl.when(pl.program_id(0) == 0)
    def _():
        r_scr[...] = jnp.zeros(r_scr.shape, F32)

    c, sa, sb = c_ref[...], sa_ref[...], sb_ref[...]
    qr[...] = _rope(r_scr[:, :HEAD_DIM], c, sa, sb) * (HEAD_DIM ** -0.5)
    kr[...] = _rope(r_scr[:, HEAD_DIM:2 * HEAD_DIM], c, sa, sb)
    vr[...] = r_scr[:, 2 * HEAD_DIM:]

    wb[:, :HEAD_DIM] = wq_ref[...].astype(BF16)
    wb[:, HEAD_DIM:2 * HEAD_DIM] = wk_ref[...].astype(BF16)
    wb[:, 2 * HEAD_DIM:] = wv_ref[...].astype(BF16)
    r_scr[...] = jnp.dot(hn_ref[...], wb[...], preferred_element_type=F32)

    _dilated_core(qr, kr, vr, o_ref, o1, o2, o3, l1, l2, l3, s_scr, p_scr, seq=seq)


def even_attention(hn, w_in, layer, tabs, batch, seq, n_heads):
    m, d = hn.shape
    n = batch * n_heads
    proj = lambda t: jnp.minimum(t, n - 1)
    attn = lambda t: jnp.maximum(t - 1, 0)
    blk = (seq, HEAD_DIM)
    wspec = lambda g: pl.BlockSpec((None, d, HEAD_DIM), lambda t: (layer, 0, g * n_heads + proj(t) % n_heads))
    tab_spec = pl.BlockSpec(blk, lambda t: (attn(t) // n_heads, 0))
    scratch = [pltpu.VMEM((d, 3 * HEAD_DIM), BF16), pltpu.VMEM((seq, 3 * HEAD_DIM), F32)]
    scratch += [pltpu.VMEM(blk, F32) for _ in range(9)]
    scratch += [pltpu.VMEM((2, DIL_W, 2 * DIL_W), F32), pltpu.VMEM((2, DIL_W, 2 * DIL_W), BF16)]
    return pl.pallas_call(
        functools.partial(_even_attn_kernel, seq=seq),
        grid=(n + 1,),
        in_specs=[pl.BlockSpec((seq, d), lambda t: (proj(t) // n_heads, 0)),
                  wspec(0), wspec(1), wspec(2), tab_spec, tab_spec, tab_spec],
        out_specs=pl.BlockSpec(blk, lambda t: (attn(t) // n_heads, attn(t) % n_heads)),
        out_shape=jax.ShapeDtypeStruct((m, n_heads * HEAD_DIM), BF16),
        scratch_shapes=scratch,
        compiler_params=_cparams(("arbitrary",)),
        name="even_attention",
    )(hn, w_in, w_in, w_in, *tabs)


def _convmod_kernel(a_ref, g_ref, ah_ref, gh_ref, cw_ref, cb_ref, lng_ref, lnb_ref, o_ref,
                    gs, cs, *, ts, ch):
    i = pl.program_id(1)
    halo = ah_ref[...] * jax.nn.sigmoid(gh_ref[...])
    halo = jnp.where(i > 0, halo, 0.0)
    cur = a_ref[...] * jax.nn.sigmoid(g_ref[...])
    for b in range(SUBLANES):
        gs[b, b:b + CONV_HALO, :] = halo
        gs[b, b + CONV_HALO:b + CONV_HALO + ts, :] = cur

    rb = 64
    first = CONV_HALO - (CONV_B_WIDTH - 1)
    for c in range(ch // LANES):
        cl = slice(c * LANES, (c + 1) * LANES)
        for r in range(ts // rb):
            acc = jnp.zeros((rb, LANES), F32)
            for k in range(CONV_B_WIDTH):
                b = -(first + k) % SUBLANES
                start = r * rb + first + k + b
                acc = acc + cw_ref[k:k + 1, cl] * gs[b, start:start + rb, cl]
            cs[r * rb:(r + 1) * rb, cl] = acc + cb_ref[:, cl]

    lrows = 32

    def ln_body(t, _):
        rs = pl.ds(pl.multiple_of(t * lrows, lrows), lrows)
        x = cs[rs, :]
        mu = jnp.mean(x, axis=-1, keepdims=True)
        xc = x - mu
        var = jnp.mean(xc * xc, axis=-1, keepdims=True)
        y = xc * lax.rsqrt(var + EPS) * lng_ref[...] + lnb_ref[...]
        o_ref[rs, :] = (y * jax.nn.sigmoid(y)).astype(o_ref.dtype)
        return 0
    lax.fori_loop(0, ts // lrows, ln_body, 0, unroll=4)


def conv_module(u, col0, ch, conv_w, conv_b, ln_g, ln_b, batch, seq):
    m = batch * seq
    ts = 256
    nt = seq // ts
    hb = ts // CONV_HALO
    ca = col0 // ch
    cur = lambda cb: pl.BlockSpec((ts, ch), lambda b, i: (b * nt + i, cb))
    halo = lambda cb: pl.BlockSpec((CONV_HALO, ch),
                                   lambda b, i: (jnp.maximum((b * nt + i) * hb - 1, 0), cb))
    vec = lambda rows: pl.BlockSpec((rows, ch), lambda b, i: (0, 0))
    return pl.pallas_call(
        functools.partial(_convmod_kernel, ts=ts, ch=ch),
        grid=(batch, nt),
        in_specs=[cur(ca), cur(ca + 1), halo(ca), halo(ca + 1),
                  vec(CONV_B_WIDTH), vec(1), vec(1), vec(1)],
        out_specs=pl.BlockSpec((ts, ch), lambda b, i: (b * nt + i, 0)),
        out_shape=jax.ShapeDtypeStruct((m, ch), BF16),
        scratch_shapes=[pltpu.VMEM((SUBLANES, CONV_HALO + ts + SUBLANES, ch), F32),
                        pltpu.VMEM((ts, ch), F32)],
        compiler_params=_cparams(("parallel", "parallel")),
        name="conv_module",
    )(u, u, u, u, conv_w, conv_b.reshape(1, ch), ln_g.reshape(1, ch), ln_b.reshape(1, ch))


def _gates_kernel(h_ref, w_ref, b_ref, f_ref, *, seq):
    w_t = w_ref[...].astype(BF16)
    w_t = jnp.concatenate([w_t, jnp.zeros((LANES - w_t.shape[0], w_t.shape[1]), BF16)], axis=0)
    fl = lax.dot_general(h_ref[...], w_t, _NT, preferred_element_type=F32) + b_ref[...]
    logf = jnp.minimum(fl, 0.0) - jnp.log1p(jnp.exp(-jnp.abs(fl)))
    blk = 128
    ii = lax.broadcasted_iota(jnp.int32, (blk, blk), 0)
    jj = lax.broadcasted_iota(jnp.int32, (blk, blk), 1)
    tri = jnp.where(jj <= ii, 1.0, 0.0).astype(BF16)
    carry = jnp.zeros((1, LANES), F32)
    for t in range(seq // blk):
        x = logf[t * blk:(t + 1) * blk, :]
        hi = x.astype(BF16)
        r1 = x - hi.astype(F32)
        mid = r1.astype(BF16)
        lo = (r1 - mid.astype(F32)).astype(BF16)
        cum = (jnp.dot(tri, hi, preferred_element_type=F32)
               + jnp.dot(tri, mid, preferred_element_type=F32)
               + jnp.dot(tri, lo, preferred_element_type=F32)) + carry
        f_ref[t * blk:(t + 1) * blk, :] = cum
        carry = cum[blk - 1:blk, :]


def forget_gates(h, w_in_t, layer, row0, n_heads, b_f, batch, seq):
    m, d = h.shape
    assert row0 % n_heads == 0
    return pl.pallas_call(
        functools.partial(_gates_kernel, seq=seq),
        grid=(batch,),
        in_specs=[pl.BlockSpec((seq, d), lambda b: (b, 0)),
                  pl.BlockSpec((None, n_heads, d), lambda b: (layer, row0 // n_heads, 0)),
                  pl.BlockSpec((1, LANES), lambda b: (0, 0))],
        out_specs=pl.BlockSpec((seq, LANES), lambda b: (b, 0)),
        out_shape=jax.ShapeDtypeStruct((m, LANES), F32),
        compiler_params=_cparams(("parallel",)),
        name="forget_gates",
    )(h, w_in_t, b_f)


def _fox_bias_columns(f_col, qa, ka, *, seq, tq):
    nblk = seq // tq
    lane = lax.broadcasted_iota(jnp.int32, (seq, LANES), 1)
    f2 = f_col * LOG2E
    base = [f2[b * tq:b * tq + 1, :] for b in range(nblk)]
    d = jnp.concatenate([f2[b * tq:(b + 1) * tq, :] - base[b] for b in range(nblk)], axis=0)
    hi = d.astype(BF16).astype(F32)
    r1 = d - hi
    mid = r1.astype(BF16).astype(F32)
    lo = (r1 - mid).astype(BF16).astype(F32)
    odd = (lane & 1) == 1
    term = jnp.where(lane < 2, hi, jnp.where(lane < 4, mid, lo))
    live = lane < 6
    qa[:, HEAD_DIM:] = jnp.where(live, jnp.where(odd, 1.0, term), 0.0).astype(BF16)
    ka[:, HEAD_DIM:] = jnp.where(live, jnp.where(odd, -term, 1.0), 0.0).astype(BF16)
    return base


def _fox_head(qa, ka, vt, base, s_scr, p_scr, store, *, seq, tq):
    nblk = seq // tq
    row = lax.broadcasted_iota(jnp.int32, (tq, tq), 0)
    col = lax.broadcasted_iota(jnp.int32, (tq, tq), 1)
    pairs = [(qi, kj) for qi in range(nblk) for kj in range(qi + 1)]

    def scores(qi, kj):
        return lax.dot_general(ka[kj * tq:(kj + 1) * tq, :], qa[qi * tq:(qi + 1) * tq, :], _NT,
                               preferred_element_type=F32)

    def softmax(t, m, l):
        qi, kj = pairs[t]
        s = s_scr[t % 2]
        if kj == qi:
            s = jnp.where(row <= col, s, NEG)
        blk_max = jnp.max(s, axis=0, keepdims=True)
        if kj != qi:
            blk_max = blk_max + (base[qi] - base[kj])
        if kj == 0:
            m_new = blk_max
            alpha = None
        else:
            m_new = jnp.maximum(m, blk_max)
            alpha = jnp.exp2(m - m_new)
        p = jnp.exp2(s - (m_new if kj == qi else m_new - (base[qi] - base[kj])))
        p_scr[t % 2] = p.astype(BF16)
        psum = jnp.sum(p, axis=0, keepdims=True)
        return m_new, (psum if kj == 0 else alpha * l + psum), alpha

    n = len(pairs)
    s_scr[0] = scores(*pairs[0])
    if n > 1:
        s_scr[1] = scores(*pairs[1])
    m, l, alpha = softmax(0, None, None)
    acc = None
    for t, (qi, kj) in enumerate(pairs):
        if t + 2 < n:
            s_scr[t % 2] = scores(*pairs[t + 2])
        pv = jnp.dot(vt[:, kj * tq:(kj + 1) * tq], p_scr[t % 2],
                     preferred_element_type=F32)
        acc = pv if kj == 0 else alpha * acc + pv
        l_t = l
        if t + 1 < n:
            m, l, alpha = softmax(t + 1, m, l)
        if kj == qi:
            store(qi, (acc * (1.0 / l_t)).T)


def _odd_mixer_kernel(hn_ref, wq_ref, wk_ref, wv_ref, fn_ref, o_ref, wb, r_scr, qa, ka, vt, s_scr, p_scr,
                      *, seq, tq, scale, pairs_per_batch):
    t = pl.program_id(0)
    d2 = 2 * HEAD_DIM

    @pl.when(t == 0)
    def _():
        r_scr[...] = jnp.zeros(r_scr.shape, F32)

    for hh in range(2):
        cs = slice(hh * HEAD_DIM, (hh + 1) * HEAD_DIM)
        qa[hh, :, :HEAD_DIM] = (r_scr[:, cs] * scale).astype(BF16)
        ka[hh, :, :HEAD_DIM] = r_scr[:, d2 + hh * HEAD_DIM:d2 + (hh + 1) * HEAD_DIM].astype(BF16)
        vt[hh] = r_scr[:, 2 * d2 + hh * HEAD_DIM:2 * d2 + (hh + 1) * HEAD_DIM].T.astype(BF16)

    wb[0:d2, :] = wq_ref[...].astype(BF16)
    wb[d2:2 * d2, :] = wk_ref[...].astype(BF16)
    wb[2 * d2:3 * d2, :] = wv_ref[...].astype(BF16)
    r_scr[...] = lax.dot_general(hn_ref[...], wb[...], _NT, preferred_element_type=F32)

    pair = jnp.maximum(t - 1, 0) % pairs_per_batch
    lane = lax.broadcasted_iota(jnp.int32, (seq, LANES), 1)
    for hh in range(2):
        f_col = jnp.sum(jnp.where(lane == 2 * pair + hh, fn_ref[...], 0.0), axis=1, keepdims=True)
        base = _fox_bias_columns(f_col, qa.at[hh], ka.at[hh], seq=seq, tq=tq)

        def store(qi, blk, hh=hh):
            o_ref[qi * tq:(qi + 1) * tq, hh * HEAD_DIM:(hh + 1) * HEAD_DIM] = blk.astype(o_ref.dtype)
        _fox_head(qa.at[hh], ka.at[hh], vt.at[hh], base, s_scr.at[hh], p_scr.at[hh], store, seq=seq, tq=tq)


def odd_mixer(hn, w_in_t, layer, f_nat, batch, seq, n_heads):
    m, d = hn.shape
    tq = 256
    d2 = 2 * HEAD_DIM
    ppb = n_heads // 2
    n_pairs = batch * ppb
    proj = lambda t: jnp.minimum(t, n_pairs - 1)
    attn = lambda t: jnp.maximum(t - 1, 0)
    wspec = lambda g: pl.BlockSpec((None, d2, d), lambda t: (layer, g * ppb + proj(t) % ppb, 0))
    return pl.pallas_call(
        functools.partial(_odd_mixer_kernel, seq=seq, tq=tq, scale=LOG2E * HEAD_DIM ** -0.5,
                          pairs_per_batch=ppb),
        grid=(n_pairs + 1,),
        in_specs=[pl.BlockSpec((seq, d), lambda t: (proj(t) // ppb, 0)),
                  wspec(0), wspec(1), wspec(2),
                  pl.BlockSpec((seq, LANES), lambda t: (attn(t) // ppb, 0))],
        out_specs=pl.BlockSpec((seq, d2), lambda t: (attn(t) // ppb, attn(t) % ppb)),
        out_shape=jax.ShapeDtypeStruct((m, n_heads * HEAD_DIM), BF16),
        scratch_shapes=[pltpu.VMEM((3 * d2, d), BF16), pltpu.VMEM((seq, 3 * d2), F32),
                        pltpu.VMEM((2, seq, d2), BF16), pltpu.VMEM((2, seq, d2), BF16),
                        pltpu.VMEM((2, HEAD_DIM, seq), BF16),
                        pltpu.VMEM((2, 2, tq, tq), F32), pltpu.VMEM((2, 2, tq, tq), BF16)],
        compiler_params=_cparams(("arbitrary",)),
        name="odd_mixer",
    )(hn, w_in_t, w_in_t, w_in_t, f_nat)


def _ffn_up_kernel(h_ref, wg_ref, wu_ref, cw_ref, o_ref, gs, us, wgb, wub, *, tm, tiles_per_seq):
    i = pl.program_id(1)

    @pl.when(i == 0)
    def _():
        wgb[...] = wg_ref[...].astype(BF16)
        wub[...] = wu_ref[...].astype(BF16)

    @pl.when(i % tiles_per_seq == 0)
    def _():
        gs[0:SUBLANES, :] = jnp.zeros((SUBLANES, gs.shape[1]), F32)

    half = tm // 2
    for r in range(2):
        rows = slice(r * half, (r + 1) * half)
        h = h_ref[rows, :]
        gs[SUBLANES + r * half:SUBLANES + (r + 1) * half, :] = jnp.dot(h, wgb[...], preferred_element_type=F32)
        us[rows, :] = jnp.dot(h, wub[...], preferred_element_type=F32)
    for r in range(2):
        lo = r * half
        conv = (cw_ref[0:1, :] * gs[lo + SUBLANES - 2:lo + SUBLANES - 2 + half, :]
                + cw_ref[1:2, :] * gs[lo + SUBLANES - 1:lo + SUBLANES - 1 + half, :]
                + cw_ref[2:3, :] * gs[lo + SUBLANES:lo + SUBLANES + half, :])
        o_ref[lo:lo + half, :] = (conv * jax.nn.sigmoid(conv) * us[lo:lo + half, :]).astype(o_ref.dtype)
    gs[0:SUBLANES, :] = gs[tm:tm + SUBLANES, :]


def ffn_up(h, w_up, conv_w, layer, d_ff, seq):
    m, d = h.shape
    tm, tn = 2048, 512
    nj = d_ff // tn
    return pl.pallas_call(
        functools.partial(_ffn_up_kernel, tm=tm, tiles_per_seq=seq // tm),
        grid=(nj, m // tm),
        in_specs=[pl.BlockSpec((tm, d), lambda j, i: (i, 0)),
                  pl.BlockSpec((None, d, tn), lambda j, i: (layer, 0, j)),
                  pl.BlockSpec((None, d, tn), lambda j, i: (layer, 0, nj + j)),
                  pl.BlockSpec((None, FFN_CONV_WIDTH, tn), lambda j, i: (layer, 0, j))],
        out_specs=pl.BlockSpec((tm, tn), lambda j, i: (i, j)),
        out_shape=jax.ShapeDtypeStruct((m, d_ff), BF16),
        scratch_shapes=[pltpu.VMEM((tm + SUBLANES, tn), F32), pltpu.VMEM((tm, tn), F32),
                        pltpu.VMEM((d, tn), BF16), pltpu.VMEM((d, tn), BF16)],
        compiler_params=_cparams(("parallel", "arbitrary")),
        name="ffn_up",
    )(h, w_up, w_up, conv_w)


def kernel(x, positions, norm_mix, norm_ffn, norm_final, ev_w_in, ev_conv_w, ev_conv_b, ev_ln_g,
           ev_ln_b, ev_w_out, od_w_in, od_b_f, od_w_out, ffn_w_up, ffn_conv_w, ffn_w_down):
    batch, seq, d = x.shape
    depth = norm_mix.shape[0]
    m = batch * seq
    c_b = ev_conv_w.shape[2]
    n_heads_a = (ev_w_in.shape[2] - 2 * c_b) // (3 * HEAD_DIM)
    n_heads_c = od_b_f.shape[1]
    d_a = n_heads_a * HEAD_DIM
    d_c = n_heads_c * HEAD_DIM
    d_ff = ffn_w_down.shape[1]

    xf = x.reshape(m, d)
    tabs = rope_tables(positions)
    od_w_in_t = jnp.swapaxes(od_w_in, 1, 2)

    for l in range(depth):
        h = rmsnorm(xf, norm_mix[l], BF16)
        if l % 2 == 0:
            e = l // 2
            y_a = even_attention(h, ev_w_in, e, tabs, batch, seq, n_heads_a)
            u = matmul([h], ev_w_in, e, 2 * c_b, F32, tm=1024, tn=1024, col0=3 * d_a)
            y_b = conv_module(u, 0, c_b, ev_conv_w[e], ev_conv_b[e], ev_ln_g[e], ev_ln_b[e],
                              batch, seq)
            xf, h = matmul([y_a, y_b], ev_w_out, e, d, F32, res=xf, tm=512, tn=d, norm_g=norm_ffn[l])
        else:
            o = l // 2
            b_f = jnp.pad(od_b_f[o], (0, LANES - n_heads_c)).reshape(1, LANES)
            f_nat = forget_gates(h, od_w_in_t, o, 3 * d_c, n_heads_c, b_f, batch, seq)
            y = odd_mixer(h, od_w_in_t, o, f_nat, batch, seq, n_heads_c)
            xf, h = matmul([y], od_w_out, o, d, F32, res=xf, tm=512, tn=d, norm_g=norm_ffn[l])
        act = ffn_up(h, ffn_w_up, ffn_conv_w, l, d_ff, seq)
        xf = matmul([act], ffn_w_down, l, d, F32, res=xf, tm=512, tn=512)
    return rmsnorm(xf, norm_final, F32).reshape(batch, seq, d)
```
